```python
import jax, jax.numpy as jnp
from jax import lax
import numpy as np

D_MODEL = 4096
BATCH = 4
SEQ = 2048
DEPTH = 1

N_ATTN_HEADS = 16
HEAD_DIM = 128
ATTN_WIDTH = N_ATTN_HEADS * HEAD_DIM
MOBA_BLOCK = 256
MOBA_TOPK = 3
QUERY_BLOCK = 128
ROPE_THETA = 10000.0
LRU_WIDTH = 2048
LRU_BLOCKS = 16
LRU_BLOCK_WIDTH = LRU_WIDTH // LRU_BLOCKS
LRU_C = 8.0
CONV_WIDTH = 4
D_FF = 11008
MACARON_WEIGHT = 0.5
NORM_EPS = 1e-6
IN_SIZES = (ATTN_WIDTH, ATTN_WIDTH, ATTN_WIDTH, LRU_WIDTH, LRU_WIDTH, D_MODEL, D_MODEL)
IN_COLS = int(sum(IN_SIZES))
IN_SPLITS = tuple(int(s) for s in np.cumsum(IN_SIZES)[:-1])

kernel_name = "hybrid_moba_rglru_macaron_block"


def rms_norm(x, g):
    xf = x.astype(jnp.float32)
    y = xf * lax.rsqrt(jnp.mean(xf * xf, axis=-1, keepdims=True) + NORM_EPS)
    return (y * g.astype(jnp.float32)).astype(x.dtype)


def swiglu(x, w_gate, w_up, w_down):
    return (jax.nn.silu(x @ w_gate) * (x @ w_up)) @ w_down


def apply_rope(t):
    s, hd = t.shape[1], t.shape[-1]
    inv_freq = ROPE_THETA ** (-jnp.arange(0, hd, 2, dtype=jnp.float32) / hd)
    ang = jnp.arange(s, dtype=jnp.float32)[:, None] * inv_freq[None, :]
    cos = jnp.cos(ang)[None, :, None, :]
    sin = jnp.sin(ang)[None, :, None, :]
    tf = t.astype(jnp.float32)
    t1, t2 = tf[..., : hd // 2], tf[..., hd // 2:]
    return jnp.concatenate([t1 * cos - t2 * sin, t2 * cos + t1 * sin], axis=-1).astype(t.dtype)


def moba_attention(q, k, v):
    b, s, h, hd = q.shape
    s_pad = -(-s // MOBA_BLOCK) * MOBA_BLOCK
    pad = ((0, 0), (0, s_pad - s), (0, 0), (0, 0))
    q, k, v = jnp.pad(q, pad), jnp.pad(k, pad), jnp.pad(v, pad)
    nb = s_pad // MOBA_BLOCK
    nqb = s_pad // QUERY_BLOCK
    q_per_kblock = MOBA_BLOCK // QUERY_BLOCK
    topk = min(MOBA_TOPK, nb)
    scale = hd ** -0.5

    kb = k.reshape(b, nb, MOBA_BLOCK, h, hd).transpose(3, 0, 1, 2, 4)
    vb = v.reshape(b, nb, MOBA_BLOCK, h, hd).transpose(3, 0, 1, 2, 4)
    k_mean = jnp.mean(kb.astype(jnp.float32), axis=3)
    qh = q.transpose(2, 0, 1, 3)

    n_past = jnp.arange(s_pad) // MOBA_BLOCK
    past = jnp.arange(nb)[None, :] < n_past[:, None]
    gate = jnp.einsum('hbsd,hbnd->hbsn', qh.astype(jnp.float32), k_mean)
    gate = jnp.where(past, gate, -jnp.inf)
    _, sel = lax.top_k(gate, topk)
    valid = sel < n_past[:, None]

    def to_steps(t):
        t = t.reshape(h, b, nqb, QUERY_BLOCK, *t.shape[3:])
        t = jnp.moveaxis(t, 2, 1)
        return t.reshape(h * nqb, b, QUERY_BLOCK, *t.shape[4:])

    head_idx = jnp.repeat(jnp.arange(h), nqb)
    qblk_idx = jnp.tile(jnp.arange(nqb), h)
    b_idx = jnp.arange(b)[:, None, None]
    n_sel = topk * MOBA_BLOCK

    def step(args):
        hi, qi, qs, ss, vs = args
        kh, vh = kb[hi], vb[hi]
        k_sel = kh[b_idx, ss]
        v_sel = vh[b_idx, ss]
        j = qi // q_per_kblock
        k_own = lax.dynamic_index_in_dim(kh, j, axis=1, keepdims=False)
        v_own = lax.dynamic_index_in_dim(vh, j, axis=1, keepdims=False)
        s_sel = jnp.einsum('bqd,bqkcd->bqkc', qs, k_sel, preferred_element_type=jnp.float32) * scale
        s_sel = jnp.where(vs[..., None], s_sel, -jnp.inf).reshape(b, QUERY_BLOCK, n_sel)
        q_pos = qi * QUERY_BLOCK + jnp.arange(QUERY_BLOCK)
        k_pos = j * MOBA_BLOCK + jnp.arange(MOBA_BLOCK)
        s_own = jnp.einsum('bqd,bcd->bqc', qs, k_own, preferred_element_type=jnp.float32) * scale
        s_own = jnp.where(k_pos[None, None, :] <= q_pos[None, :, None], s_own, -jnp.inf)
        p = jax.nn.softmax(jnp.concatenate([s_sel, s_own], axis=-1), axis=-1)
        p_sel = p[..., :n_sel].reshape(b, QUERY_BLOCK, topk, MOBA_BLOCK).astype(v.dtype)
        p_own = p[..., n_sel:].astype(v.dtype)
        return (jnp.einsum('bqkc,bqkcd->bqd', p_sel, v_sel)
                + jnp.einsum('bqc,bcd->bqd', p_own, v_own))

    out = lax.map(step, (head_idx, qblk_idx, to_steps(qh), to_steps(sel), to_steps(valid)))
    out = out.reshape(h, nqb, b, QUERY_BLOCK, hd).transpose(2, 1, 3, 0, 4).reshape(b, s_pad, h * hd)
    return out[:, :s]


def causal_depthwise_conv(x, w, bias):
    s = x.shape[1]
    xp = jnp.pad(x, ((0, 0), (CONV_WIDTH - 1, 0), (0, 0)))
    y = bias
    for tap in range(CONV_WIDTH):
        y = y + xp[:, tap:tap + s] * w[tap]
    return y


def _linear_combine(c1, c2):
    a1, b1 = c1
    a2, b2 = c2
    return a1 * a2, a2 * b1 + b2


def rg_lru(x, w_a, b_a, w_x, b_x, lam):
    b, s, w = x.shape
    xb = x.reshape(b, s, LRU_BLOCKS, LRU_BLOCK_WIDTH)
    r = jax.nn.sigmoid((jnp.einsum('bsnc,ncd->bsnd', xb, w_a).reshape(b, s, w) + b_a).astype(jnp.float32))
    i = jax.nn.sigmoid((jnp.einsum('bsnc,ncd->bsnd', xb, w_x).reshape(b, s, w) + b_x).astype(jnp.float32))
    log_a = -LRU_C * r * jax.nn.softplus(-lam.astype(jnp.float32))
    a = jnp.exp(log_a)
    u = x.astype(jnp.float32) * i * jnp.sqrt(-jnp.expm1(2.0 * log_a))
    _, hs = lax.associative_scan(_linear_combine, (a, u), axis=1)
    return hs.astype(x.dtype)


def hybrid_mixer(hn, w_in, conv_w, conv_b, rg_w_a, rg_b_a, rg_w_x, rg_b_x, lru_lambda,
                 w_attn_out, w_rec_out, w_o):
    b, s, _ = hn.shape
    proj = hn @ w_in
    q, k, v, x_rec, x_gate, g_a, g_b = jnp.split(proj, IN_SPLITS, axis=-1)
    q = apply_rope(q.reshape(b, s, N_ATTN_HEADS, HEAD_DIM))
    k = apply_rope(k.reshape(b, s, N_ATTN_HEADS, HEAD_DIM))
    v = v.reshape(b, s, N_ATTN_HEADS, HEAD_DIM)
    y_a = moba_attention(q, k, v) @ w_attn_out
    x_rec = causal_depthwise_conv(x_rec, conv_w, conv_b)
    y_rec = rg_lru(x_rec, rg_w_a, rg_b_a, rg_w_x, rg_b_x, lru_lambda) * jax.nn.gelu(x_gate)
    y_b = y_rec @ w_rec_out
    merged = jax.nn.sigmoid(g_a) * y_a + jax.nn.sigmoid(g_b) * y_b
    return merged @ w_o


def setup_inputs(seed: int = 0) -> dict:
    key = jax.random.key(seed)
    ks = jax.random.split(key, 26)
    f32 = jnp.float32

    def normal(k, shape, fan_in):
        return jax.random.normal(k, (DEPTH,) + shape, f32) * (fan_in ** -0.5)

    def gain(k):
        return 1.0 + 0.05 * jax.random.normal(k, (DEPTH, D_MODEL), f32)

    def small(k, shape):
        return 0.01 * jax.random.normal(k, (DEPTH,) + shape, f32)

    a8 = jax.random.uniform(ks[14], (DEPTH, LRU_WIDTH), f32, 0.9, 0.999)
    a_base = a8 ** (1.0 / LRU_C)
    lru_lambda = jnp.log(a_base) - jnp.log1p(-a_base)
    return {
        'x': jax.random.normal(ks[0], (BATCH, SEQ, D_MODEL), f32),
        'ffn1_pre_g': gain(ks[1]),
        'ffn1_w_gate': normal(ks[2], (D_MODEL, D_FF), D_MODEL),
        'ffn1_w_up': normal(ks[3], (D_MODEL, D_FF), D_MODEL),
        'ffn1_w_down': normal(ks[4], (D_FF, D_MODEL), D_FF),
        'ffn1_post_g': gain(ks[5]),
        'mix_pre_g': gain(ks[6]),
        'w_in': normal(ks[7], (D_MODEL, IN_COLS), D_MODEL),
        'conv_w': normal(ks[8], (CONV_WIDTH, LRU_WIDTH), CONV_WIDTH),
        'conv_b': small(ks[9], (LRU_WIDTH,)),
        'rg_w_a': normal(ks[10], (LRU_BLOCKS, LRU_BLOCK_WIDTH, LRU_BLOCK_WIDTH), LRU_BLOCK_WIDTH),
        'rg_b_a': small(ks[11], (LRU_WIDTH,)),
        'rg_w_x': normal(ks[12], (LRU_BLOCKS, LRU_BLOCK_WIDTH, LRU_BLOCK_WIDTH), LRU_BLOCK_WIDTH),
        'rg_b_x': small(ks[13], (LRU_WIDTH,)),
        'lru_lambda': lru_lambda,
        'w_attn_out': normal(ks[15], (ATTN_WIDTH, D_MODEL), ATTN_WIDTH),
        'w_rec_out': normal(ks[16], (LRU_WIDTH, D_MODEL), LRU_WIDTH),
        'w_o': normal(ks[17], (D_MODEL, D_MODEL), D_MODEL),
        'mix_post_g': gain(ks[18]),
        'ffn2_pre_g': gain(ks[19]),
        'ffn2_w_gate': normal(ks[20], (D_MODEL, D_FF), D_MODEL),
        'ffn2_w_up': normal(ks[21], (D_MODEL, D_FF), D_MODEL),
        'ffn2_w_down': normal(ks[22], (D_FF, D_MODEL), D_FF),
        'ffn2_post_g': gain(ks[23]),
    }


def reference(x, ffn1_pre_g, ffn1_w_gate, ffn1_w_up, ffn1_w_down, ffn1_post_g,
              mix_pre_g, w_in, conv_w, conv_b, rg_w_a, rg_b_a, rg_w_x, rg_b_x, lru_lambda,
              w_attn_out, w_rec_out, w_o, mix_post_g,
              ffn2_pre_g, ffn2_w_gate, ffn2_w_up, ffn2_w_down, ffn2_post_g):
    for l in range(DEPTH):
        f = swiglu(rms_norm(x, ffn1_pre_g[l]), ffn1_w_gate[l], ffn1_w_up[l], ffn1_w_down[l])
        x = x + MACARON_WEIGHT * rms_norm(f, ffn1_post_g[l])
        m = hybrid_mixer(rms_norm(x, mix_pre_g[l]), w_in[l], conv_w[l], conv_b[l],
                         rg_w_a[l], rg_b_a[l], rg_w_x[l], rg_b_x[l], lru_lambda[l],
                         w_attn_out[l], w_rec_out[l], w_o[l])
        x = x + rms_norm(m, mix_post_g[l])
        f = swiglu(rms_norm(x, ffn2_pre_g[l]), ffn2_w_gate[l], ffn2_w_up[l], ffn2_w_down[l])
        x = x + MACARON_WEIGHT * rms_norm(f, ffn2_post_g[l])
    return x
```

```python
import functools
import math

import jax
import jax.numpy as jnp
from jax import lax
from jax.experimental import pallas as pl
from jax.experimental.pallas import tpu as pltpu

D_MODEL = 4096
BATCH = 4
SEQ = 2048
TOKENS = BATCH * SEQ

N_ATTN_HEADS = 16
HEAD_DIM = 128
ATTN_WIDTH = N_ATTN_HEADS * HEAD_DIM
MOBA_BLOCK = 256
MOBA_TOPK = 3
N_KV_BLOCKS = SEQ // MOBA_BLOCK
ROPE_THETA = 10000.0

LRU_WIDTH = 2048
LRU_BLOCKS = 16
LRU_BLOCK_WIDTH = LRU_WIDTH // LRU_BLOCKS
LRU_C = 8.0
CONV_WIDTH = 4

D_FF = 11008
MACARON_WEIGHT = 0.5
NORM_EPS = 1e-6

V7X_VMEM_LIMIT_BYTES = 56 * 1024 * 1024
LANES = 128
SUBLANES = 8

F32 = jnp.float32
BF16 = jnp.bfloat16


def _params(semantics):
    return pltpu.CompilerParams(dimension_semantics=semantics,
                                vmem_limit_bytes=V7X_VMEM_LIMIT_BYTES)


def _rms(x, g):
    return x * lax.rsqrt(jnp.mean(x * x, axis=-1, keepdims=True) + NORM_EPS) * g


def _prenorm_kernel(x_ref, g_ref, o_ref):
    o_ref[...] = _rms(x_ref[...], g_ref[...]).astype(o_ref.dtype)


def _prenorm(x, g, *, tm=256):
    m, d = x.shape
    return pl.pallas_call(
        _prenorm_kernel,
        name="prenorm",
        grid=(m // tm,),
        in_specs=[pl.BlockSpec((tm, d), lambda i: (i, 0)),
                  pl.BlockSpec((1, d), lambda i: (0, 0))],
        out_specs=pl.BlockSpec((tm, d), lambda i: (i, 0)),
        out_shape=jax.ShapeDtypeStruct((m, d), BF16),
        compiler_params=_params(("parallel",)),
    )(x, g)


def _residual_norm_kernel(x_ref, f_ref, pg_ref, ng_ref, xo_ref, hi_ref, lo_ref, *, weight):
    xn = x_ref[...] + weight * _rms(f_ref[...], pg_ref[...])
    xo_ref[...] = xn
    z = _rms(xn, ng_ref[...])
    hi = z.astype(BF16)
    hi_ref[...] = hi
    lo_ref[...] = (z - hi.astype(F32)).astype(BF16)


def _residual_norm(x, f, post_g, next_g, *, weight, tm=256):
    m, d = x.shape
    row = pl.BlockSpec((tm, d), lambda i: (i, 0))
    vec = pl.BlockSpec((1, d), lambda i: (0, 0))
    return pl.pallas_call(
        functools.partial(_residual_norm_kernel, weight=weight),
        name="residual_norm",
        grid=(m // tm,),
        in_specs=[row, row, vec, vec],
        out_specs=[row, row, row],
        out_shape=[jax.ShapeDtypeStruct((m, d), F32),
                   jax.ShapeDtypeStruct((m, d), BF16),
                   jax.ShapeDtypeStruct((m, d), BF16)],
        compiler_params=_params(("parallel",)),
    )(x, f, post_g, next_g)


def _residual_kernel(x_ref, f_ref, pg_ref, xo_ref, *, weight):
    xo_ref[...] = x_ref[...] + weight * _rms(f_ref[...], pg_ref[...])


def _residual(x, f, post_g, *, weight, tm=256):
    m, d = x.shape
    row = pl.BlockSpec((tm, d), lambda i: (i, 0))
    vec = pl.BlockSpec((1, d), lambda i: (0, 0))
    return pl.pallas_call(
        functools.partial(_residual_kernel, weight=weight),
        name="residual_out",
        grid=(m // tm,),
        in_specs=[row, row, vec],
        out_specs=row,
        out_shape=jax.ShapeDtypeStruct((m, d), F32),
        compiler_params=_params(("parallel",)),
    )(x, f, post_g)


def _mm_kernel(*refs, n_a, n_w, n_e, pairs, epilogue):
    a_refs = refs[:n_a]
    w_refs = refs[n_a:n_a + n_w]
    e_refs = refs[n_a + n_w:n_a + n_w + n_e]
    o_refs = refs[n_a + n_w + n_e:]
    dots = [jnp.dot(a_refs[ai][...], w_refs[wi][...], preferred_element_type=F32)
            for ai, wi in pairs]
    outs = epilogue(dots, [e[...] for e in e_refs])
    for o_ref, val in zip(o_refs, outs):
        o_ref[...] = val.astype(o_ref.dtype)


def _mm(name, a_list, w_list, pairs, extras, epilogue, out_dtypes, *, tm, tn, n_cols):
    m = a_list[0].shape[0]
    grid = (m // tm, n_cols // tn)
    in_specs, operands = [], []
    for a in a_list:
        in_specs.append(pl.BlockSpec((tm, a.shape[1]), lambda i, j: (i, 0)))
        operands.append(a)
    for w, col0 in w_list:
        in_specs.append(pl.BlockSpec((w.shape[0], tn), lambda i, j, c=col0 // tn: (0, j + c)))
        operands.append(w)
    for kind, arr, *rest in extras:
        if kind == 'mn':
            in_specs.append(pl.BlockSpec((tm, tn), lambda i, j, c=rest[0] // tn: (i, j + c)))
        elif kind == 'n':
            in_specs.append(pl.BlockSpec((1, tn), lambda i, j, c=rest[0] // tn: (0, j + c)))
        else:
            in_specs.append(pl.BlockSpec((tm, LANES), lambda i, j, p=SEQ // tm: (i % p, 0)))
        operands.append(arr)
    kernel = functools.partial(_mm_kernel, n_a=len(a_list), n_w=len(w_list),
                               n_e=len(extras), pairs=tuple(pairs), epilogue=epilogue)
    outs = pl.pallas_call(
        kernel,
        name=name,
        grid=grid,
        in_specs=in_specs,
        out_specs=[pl.BlockSpec((tm, tn), lambda i, j: (i, j)) for _ in out_dtypes],
        out_shape=[jax.ShapeDtypeStruct((m, n_cols), dt) for dt in out_dtypes],
        compiler_params=_params(("parallel", "arbitrary")),
    )(*operands)
    return outs


def _ep_swiglu(dots, extras):
    g, u = dots
    return (g * jax.nn.sigmoid(g) * u,)


def _ep_sum(dots, extras):
    acc = dots[0]
    for d in dots[1:]:
        acc = acc + d
    return (acc,)


def _ep_rope(dots, extras):
    acc = dots[0] + dots[1] + dots[2]
    cos, sin_signed = extras
    parts = []
    for h in range(acc.shape[1] // HEAD_DIM):
        t = acc[:, h * HEAD_DIM:(h + 1) * HEAD_DIM]
        parts.append(t * cos + pltpu.roll(t, HEAD_DIM // 2, axis=1) * sin_signed)
    return (jnp.concatenate(parts, axis=1),)


def _ep_sigmoid(dots, extras):
    return (jax.nn.sigmoid(dots[0]),)


def _ep_merge(dots, extras):
    y_a, y_b = dots
    s_a, s_b = extras
    return (s_a.astype(F32) * y_a + s_b.astype(F32) * y_b,)


def _moba_kernel(q_ref, k_ref, v_ref, o_ref, kb_ref, km_ref):
    tq = MOBA_BLOCK
    qb = pl.program_id(2)

    @pl.when(qb == 0)
    def _():
        kb_ref[...] = k_ref[...].astype(BF16)
        for n in range(N_KV_BLOCKS):
            blk = k_ref[n * MOBA_BLOCK:(n + 1) * MOBA_BLOCK, :]
            km_ref[n:n + 1, :] = jnp.sum(blk, axis=0, keepdims=True) * (1.0 / MOBA_BLOCK)

    q = q_ref[...]
    gate = lax.dot_general(km_ref[...], q, (((1,), (1,)), ((), ())),
                           precision=lax.Precision.HIGHEST,
                           preferred_element_type=F32)
    row = lax.broadcasted_iota(jnp.int32, (N_KV_BLOCKS, tq), 0)
    past = row < qb
    gate = jnp.where(past, gate, -jnp.inf)
    rank = jnp.zeros((N_KV_BLOCKS, tq), F32)
    for mth in range(N_KV_BLOCKS):
        gm = gate[mth:mth + 1, :]
        beats = (gm > gate) | ((gm == gate) & (mth < row))
        rank = rank + beats.astype(F32)
    member = ((rank < MOBA_TOPK) & past).astype(BF16)
    member = jnp.concatenate(
        [member, jnp.zeros((LANES - N_KV_BLOCKS, tq), BF16)], axis=0)
    eye = (lax.broadcasted_iota(jnp.int32, (tq, tq), 0)
           == lax.broadcasted_iota(jnp.int32, (tq, tq), 1)).astype(BF16)
    member_col = lax.dot_general(eye, member, (((1,), (1,)), ((), ())),
                                 preferred_element_type=F32)
    lane = lax.broadcasted_iota(jnp.int32, (tq, LANES), 1)

    qs = (q * (HEAD_DIM ** -0.5)).astype(BF16)

    off_own = pl.multiple_of(qb * MOBA_BLOCK, MOBA_BLOCK)
    k_own = kb_ref[pl.ds(off_own, MOBA_BLOCK), :]
    v_own = v_ref[pl.ds(off_own, MOBA_BLOCK), :]
    s = lax.dot_general(qs, k_own, (((1,), (1,)), ((), ())), preferred_element_type=F32)
    r_idx = lax.broadcasted_iota(jnp.int32, (tq, MOBA_BLOCK), 0)
    c_idx = lax.broadcasted_iota(jnp.int32, (tq, MOBA_BLOCK), 1)
    s = jnp.where(c_idx <= r_idx, s, -jnp.inf)
    m0 = jnp.max(s, axis=1, keepdims=True)
    p = jnp.exp(s - m0)
    l0 = jnp.sum(p, axis=1, keepdims=True)
    acc0 = jnp.dot(p.astype(BF16), v_own, preferred_element_type=F32)

    def body(n, carry):
        m_run, l_run, acc = carry
        off = pl.multiple_of(n * MOBA_BLOCK, MOBA_BLOCK)
        kn = kb_ref[pl.ds(off, MOBA_BLOCK), :]
        vn = v_ref[pl.ds(off, MOBA_BLOCK), :]
        sn = lax.dot_general(qs, kn, (((1,), (1,)), ((), ())), preferred_element_type=F32)
        sel = jnp.sum(jnp.where(lane == n, member_col, 0.0), axis=1, keepdims=True) > 0.5
        sn = jnp.where(sel, sn, -jnp.inf)
        m_new = jnp.maximum(m_run, jnp.max(sn, axis=1, keepdims=True))
        alpha = jnp.exp(m_run - m_new)
        pn = jnp.exp(sn - m_new)
        l_new = alpha * l_run + jnp.sum(pn, axis=1, keepdims=True)
        acc_new = alpha * acc + jnp.dot(pn.astype(BF16), vn, preferred_element_type=F32)
        return m_new, l_new, acc_new

    _, l_fin, acc_fin = lax.fori_loop(0, qb, body, (m0, l0, acc0))
    o_ref[...] = (acc_fin / l_fin).astype(o_ref.dtype)


def _moba_attention(qk, v):
    n_qb = SEQ // MOBA_BLOCK
    return pl.pallas_call(
        _moba_kernel,
        name="moba_attention",
        grid=(BATCH, N_ATTN_HEADS, n_qb),
        in_specs=[
            pl.BlockSpec((MOBA_BLOCK, HEAD_DIM), lambda b, h, t: (b * n_qb + t, h)),
            pl.BlockSpec((SEQ, HEAD_DIM), lambda b, h, t: (b, N_ATTN_HEADS + h)),
            pl.BlockSpec((SEQ, HEAD_DIM), lambda b, h, t: (b, h)),
        ],
        out_specs=pl.BlockSpec((MOBA_BLOCK, HEAD_DIM), lambda b, h, t: (b * n_qb + t, h)),
        out_shape=jax.ShapeDtypeStruct((TOKENS, ATTN_WIDTH), BF16),
        scratch_shapes=[pltpu.VMEM((SEQ, HEAD_DIM), BF16),
                        pltpu.VMEM((N_KV_BLOCKS, HEAD_DIM), F32)],
        compiler_params=_params(("arbitrary", "arbitrary", "arbitrary")),
    )(qk, qk, v)


RG_TILE = 512
RG_CHUNK = 256
CONV_PAD = SUBLANES


def _rglru_kernel(x_ref, xg_ref, cw_ref, cb_ref, wa_ref, ba_ref, wx_ref, bx_ref, lam_ref,
                  o_ref, xp_ref, a_ref, u_ref):
    wt = RG_TILE
    xp_ref[0:CONV_PAD, :] = jnp.zeros((CONV_PAD, wt), F32)
    xp_ref[CONV_PAD:, :] = x_ref[...]

    z = -lam_ref[...]
    softplus = jnp.maximum(z, 0.0) + jnp.log1p(jnp.exp(-jnp.abs(z)))
    cw = cw_ref[...]
    cb = cb_ref[...]
    for c in range(SEQ // RG_CHUNK):
        r0 = c * RG_CHUNK
        y = jnp.broadcast_to(cb, (RG_CHUNK, wt))
        for tap in range(CONV_WIDTH):
            start = r0 + CONV_PAD - (CONV_WIDTH - 1) + tap
            y = y + xp_ref[start:start + RG_CHUNK, :] * cw[tap:tap + 1, :]
        yb = y.astype(BF16)
        for n in range(wt // LRU_BLOCK_WIDTH):
            sl = slice(n * LRU_BLOCK_WIDTH, (n + 1) * LRU_BLOCK_WIDTH)
            ra = jnp.dot(yb[:, sl], wa_ref[n].astype(BF16), preferred_element_type=F32)
            rx = jnp.dot(yb[:, sl], wx_ref[n].astype(BF16), preferred_element_type=F32)
            r = jax.nn.sigmoid(ra + ba_ref[:, sl])
            i = jax.nn.sigmoid(rx + bx_ref[:, sl])
            log_a = -LRU_C * r * softplus[:, sl]
            a_ref[r0:r0 + RG_CHUNK, sl] = jnp.exp(log_a)
            t = jnp.tanh(log_a)
            u_ref[r0:r0 + RG_CHUNK, sl] = y[:, sl] * i * jnp.sqrt(-2.0 * t / (1.0 - t))

    row8 = lax.broadcasted_iota(jnp.int32, (SUBLANES, wt), 0)

    def group(g, h_prev):
        r0 = pl.multiple_of(g * SUBLANES, SUBLANES)
        a = a_ref[pl.ds(r0, SUBLANES), :]
        u = u_ref[pl.ds(r0, SUBLANES), :]
        for d in (1, 2, 4):
            keep = row8 >= d
            a_s = jnp.where(keep, pltpu.roll(a, d, axis=0), 1.0)
            u_s = jnp.where(keep, pltpu.roll(u, d, axis=0), 0.0)
            u = u + a * u_s
            a = a * a_s
        h = u + a * h_prev
        u_ref[pl.ds(r0, SUBLANES), :] = h
        return jnp.broadcast_to(h[SUBLANES - 1:SUBLANES, :], (SUBLANES, wt))

    lax.fori_loop(0, SEQ // SUBLANES, group, jnp.zeros((SUBLANES, wt), F32), unroll=4)

    for c in range(SEQ // RG_CHUNK):
        r0 = c * RG_CHUNK
        gate = jax.nn.gelu(xg_ref[r0:r0 + RG_CHUNK, :])
        o_ref[r0:r0 + RG_CHUNK, :] = (u_ref[r0:r0 + RG_CHUNK, :] * gate).astype(o_ref.dtype)


def _rglru(rec_gate, conv_w, conv_b, w_a, b_a, w_x, b_x, lam):
    wt = RG_TILE
    n_wt = LRU_WIDTH // wt
    nb = wt // LRU_BLOCK_WIDTH
    vec = pl.BlockSpec((1, wt), lambda b, j: (0, j))
    blk = pl.BlockSpec((nb, LRU_BLOCK_WIDTH, LRU_BLOCK_WIDTH), lambda b, j: (j, 0, 0))
    return pl.pallas_call(
        _rglru_kernel,
        name="conv_rglru",
        grid=(BATCH, n_wt),
        in_specs=[
            pl.BlockSpec((SEQ, wt), lambda b, j: (b, j)),
            pl.BlockSpec((SEQ, wt), lambda b, j: (b, j + n_wt)),
            pl.BlockSpec((CONV_WIDTH, wt), lambda b, j: (0, j)),
            vec, blk, vec, blk, vec, vec,
        ],
        out_specs=pl.BlockSpec((SEQ, wt), lambda b, j: (b, j)),
        out_shape=jax.ShapeDtypeStruct((TOKENS, LRU_WIDTH), BF16),
        scratch_shapes=[pltpu.VMEM((SEQ + CONV_PAD, wt), F32),
                        pltpu.VMEM((SEQ, wt), F32),
                        pltpu.VMEM((SEQ, wt), F32)],
        compiler_params=_params(("parallel", "parallel")),
    )(rec_gate, rec_gate, conv_w, conv_b, w_a, b_a, w_x, b_x, lam)


def _rope_tables():
    inv_freq = ROPE_THETA ** (-jnp.arange(0, HEAD_DIM, 2, dtype=F32) / HEAD_DIM)
    ang = jnp.arange(SEQ, dtype=F32)[:, None] * inv_freq[None, :]
    cos, sin = jnp.cos(ang), jnp.sin(ang)
    return jnp.concatenate([cos, cos], axis=1), jnp.concatenate([-sin, sin], axis=1)


def _split_bf16(w):
    hi = w.astype(BF16)
    return hi, (w - hi.astype(F32)).astype(BF16)


def _ffn(xn, w_gate, w_up, w_down):
    (h,) = _mm("ffn_up", [xn], [(w_gate.astype(BF16), 0), (w_up.astype(BF16), 0)],
               [(0, 0), (0, 1)], [], _ep_swiglu, [BF16], tm=1024, tn=256, n_cols=D_FF)
    (f,) = _mm("ffn_down", [h], [(w_down.astype(BF16), 0)], [(0, 0)], [], _ep_sum, [F32],
               tm=512, tn=512, n_cols=D_MODEL)
    return f


def kernel(x, ffn1_pre_g, ffn1_w_gate, ffn1_w_up, ffn1_w_down, ffn1_post_g, mix_pre_g, w_in, conv_w, conv_b, rg_w_a, rg_b_a, rg_w_x, rg_b_x, lru_lambda, w_attn_out, w_rec_out, w_o, mix_post_g, ffn2_pre_g, ffn2_w_gate, ffn2_w_up, ffn2_w_down, ffn2_post_g):
    depth = ffn1_pre_g.shape[0]
    cos_t, sin_t = _rope_tables()
    xf = x.reshape(TOKENS, D_MODEL)
    row = lambda p: p.reshape(1, -1)
    c_q, c_v, c_rec, c_gate = 0, 2 * ATTN_WIDTH, 3 * ATTN_WIDTH, 3 * ATTN_WIDTH + 2 * LRU_WIDTH
    for l in range(depth):
        xn = _prenorm(xf, row(ffn1_pre_g[l]))
        f = _ffn(xn, ffn1_w_gate[l], ffn1_w_up[l], ffn1_w_down[l])
        xf, hn_hi, hn_lo = _residual_norm(xf, f, row(ffn1_post_g[l]), row(mix_pre_g[l]),
                                          weight=MACARON_WEIGHT)
        w_in_bf = w_in[l].astype(BF16)
        wqk_hi, wqk_lo = _split_bf16(w_in[l][:, :2 * ATTN_WIDTH])
        (qk,) = _mm("proj_qk_rope", [hn_hi, hn_lo], [(wqk_hi, c_q), (wqk_lo, c_q)],
                    [(0, 0), (0, 1), (1, 0)], [('pos', cos_t), ('pos', sin_t)], _ep_rope, [F32],
                    tm=512, tn=512, n_cols=2 * ATTN_WIDTH)
        (v,) = _mm("proj_v", [hn_hi], [(w_in_bf, c_v)], [(0, 0)], [], _ep_sum, [BF16],
                   tm=1024, tn=1024, n_cols=ATTN_WIDTH)
        (rec_gate,) = _mm("proj_rec", [hn_hi], [(w_in_bf, c_rec)], [(0, 0)], [], _ep_sum, [F32],
                          tm=1024, tn=1024, n_cols=2 * LRU_WIDTH)
        (merge_gates,) = _mm("proj_merge_gates", [hn_hi], [(w_in_bf, c_gate)], [(0, 0)], [],
                             _ep_sigmoid, [BF16], tm=1024, tn=1024, n_cols=2 * D_MODEL)
        attn = _moba_attention(qk, v)
        y_rec = _rglru(rec_gate, conv_w[l], row(conv_b[l]), rg_w_a[l], row(rg_b_a[l]),
                       rg_w_x[l], row(rg_b_x[l]), row(lru_lambda[l]))
        (merged,) = _mm("mixer_out_merge", [attn, y_rec],
                        [(w_attn_out[l].astype(BF16), 0), (w_rec_out[l].astype(BF16), 0)],
                        [(0, 0), (1, 1)],
                        [('mn', merge_gates, 0), ('mn', merge_gates, D_MODEL)],
                        _ep_merge, [BF16], tm=1024, tn=512, n_cols=D_MODEL)
        (mix,) = _mm("proj_o", [merged], [(w_o[l].astype(BF16), 0)], [(0, 0)], [], _ep_sum,
                     [F32], tm=1024, tn=1024, n_cols=D_MODEL)
        xf, xn2, _ = _residual_norm(xf, mix, row(mix_post_g[l]), row(ffn2_pre_g[l]), weight=1.0)
        f = _ffn(xn2, ffn2_w_gate[l], ffn2_w_up[l], ffn2_w_down[l])
        xf = _residual(xf, f, row(ffn2_post_g[l]), weight=MACARON_WEIGHT)
    return xf.reshape(BATCH, SEQ, D_MODEL)
```

```python
import functools

import jax
import jax.numpy as jnp
from jax import lax
from jax.experimental import pallas as pl
from jax.experimental.pallas import tpu as pltpu

D_MODEL = 4096
BATCH = 4
SEQ = 2048
TOKENS = BATCH * SEQ

N_ATTN_HEADS = 16
HEAD_DIM = 128
ATTN_WIDTH = N_ATTN_HEADS * HEAD_DIM
MOBA_BLOCK = 256
MOBA_TOPK = 3
N_KV_BLOCKS = SEQ // MOBA_BLOCK
ROPE_THETA = 10000.0

LRU_WIDTH = 2048
LRU_BLOCKS = 16
LRU_BLOCK_WIDTH = LRU_WIDTH // LRU_BLOCKS
LRU_C = 8.0
CONV_WIDTH = 4

D_FF = 11008
MACARON_WEIGHT = 0.5
NORM_EPS = 1e-6

V7X_VMEM_LIMIT_BYTES = 56 * 1024 * 1024
LANES = 128
SUBLANES = 8

F32 = jnp.float32
BF16 = jnp.bfloat16


def _params(semantics):
    return pltpu.CompilerParams(dimension_semantics=semantics,
                                vmem_limit_bytes=V7X_VMEM_LIMIT_BYTES)


def _rms(x, g):
    return x * lax.rsqrt(jnp.mean(x * x, axis=-1, keepdims=True) + NORM_EPS) * g


def _prenorm_kernel(x_ref, g_ref, o_ref):
    o_ref[...] = _rms(x_ref[...], g_ref[...]).astype(o_ref.dtype)


def _prenorm(x, g, *, tm=256):
    m, d = x.shape
    return pl.pallas_call(
        _prenorm_kernel,
        name="prenorm",
        grid=(m // tm,),
        in_specs=[pl.BlockSpec((tm, d), lambda i: (i, 0)),
                  pl.BlockSpec((1, d), lambda i: (0, 0))],
        out_specs=pl.BlockSpec((tm, d), lambda i: (i, 0)),
        out_shape=jax.ShapeDtypeStruct((m, d), BF16),
        compiler_params=_params(("parallel",)),
    )(x, g)


def _residual_norm_kernel(x_ref, f_ref, pg_ref, ng_ref, xo_ref, xn_ref, *, weight):
    xn = x_ref[...] + weight * _rms(f_ref[...], pg_ref[...])
    xo_ref[...] = xn
    xn_ref[...] = _rms(xn, ng_ref[...]).astype(xn_ref.dtype)


def _residual_norm(x, f, post_g, next_g, *, weight, tm=256):
    m, d = x.shape
    row = pl.BlockSpec((tm, d), lambda i: (i, 0))
    vec = pl.BlockSpec((1, d), lambda i: (0, 0))
    return pl.pallas_call(
        functools.partial(_residual_norm_kernel, weight=weight),
        name="residual_norm",
        grid=(m // tm,),
        in_specs=[row, row, vec, vec],
        out_specs=[row, row],
        out_shape=[jax.ShapeDtypeStruct((m, d), F32),
                   jax.ShapeDtypeStruct((m, d), BF16)],
        compiler_params=_params(("parallel",)),
    )(x, f, post_g, next_g)


def _residual_kernel(x_ref, f_ref, pg_ref, xo_ref, *, weight):
    xo_ref[...] = x_ref[...] + weight * _rms(f_ref[...], pg_ref[...])


def _residual(x, f, post_g, *, weight, tm=256):
    m, d = x.shape
    row = pl.BlockSpec((tm, d), lambda i: (i, 0))
    vec = pl.BlockSpec((1, d), lambda i: (0, 0))
    return pl.pallas_call(
        functools.partial(_residual_kernel, weight=weight),
        name="residual_out",
        grid=(m // tm,),
        in_specs=[row, row, vec],
        out_specs=row,
        out_shape=jax.ShapeDtypeStruct((m, d), F32),
        compiler_params=_params(("parallel",)),
    )(x, f, post_g)


def _mm_kernel(*refs, n_a, n_w, n_e, pairs, epilogue):
    a_refs = refs[:n_a]
    w_refs = refs[n_a:n_a + n_w]
    e_refs = refs[n_a + n_w:n_a + n_w + n_e]
    o_refs = refs[n_a + n_w + n_e:]
    w_tiles = [w_ref[...].astype(BF16) for w_ref in w_refs]
    dots = [jnp.dot(a_refs[ai][...], w_tiles[wi], preferred_element_type=F32)
            for ai, wi in pairs]
    outs = epilogue(dots, [e[...] for e in e_refs])
    for o_ref, val in zip(o_refs, outs):
        o_ref[...] = val.astype(o_ref.dtype)


def _mm(name, a_list, w_list, pairs, extras, epilogue, out_dtypes, *, tm, tn, n_cols,
        a_buffers=2):
    m = a_list[0].shape[0]
    grid = (m // tm, n_cols // tn)
    in_specs, operands = [], []
    a_mode = {} if a_buffers == 2 else {"pipeline_mode": pl.Buffered(a_buffers)}
    for a in a_list:
        in_specs.append(pl.BlockSpec((tm, a.shape[1]), lambda i, j: (i, 0), **a_mode))
        operands.append(a)
    for w, col0 in w_list:
        in_specs.append(pl.BlockSpec((w.shape[0], tn), lambda i, j, c=col0 // tn: (0, j + c)))
        operands.append(w)
    for kind, arr, *rest in extras:
        if kind == 'mn':
            in_specs.append(pl.BlockSpec((tm, tn), lambda i, j, c=rest[0] // tn: (i, j + c)))
        else:
            in_specs.append(pl.BlockSpec((tm, LANES), lambda i, j, p=SEQ // tm: (i % p, 0)))
        operands.append(arr)
    kernel = functools.partial(_mm_kernel, n_a=len(a_list), n_w=len(w_list),
                               n_e=len(extras), pairs=tuple(pairs), epilogue=epilogue)
    outs = pl.pallas_call(
        kernel,
        name=name,
        grid=grid,
        in_specs=in_specs,
        out_specs=[pl.BlockSpec((tm, tn), lambda i, j: (i, j)) for _ in out_dtypes],
        out_shape=[jax.ShapeDtypeStruct((m, n_cols), dt) for dt in out_dtypes],
        compiler_params=_params(("parallel", "arbitrary")),
    )(*operands)
    return outs


def _ep_swiglu(dots, extras):
    g, u = dots
    return (g * jax.nn.sigmoid(g) * u,)


def _ep_identity(dots, extras):
    return (dots[0],)


def _ep_rope(dots, extras):
    acc = dots[0]
    cos, sin_signed = extras
    parts = []
    for h in range(acc.shape[1] // HEAD_DIM):
        t = acc[:, h * HEAD_DIM:(h + 1) * HEAD_DIM]
        parts.append(t * cos + pltpu.roll(t, HEAD_DIM // 2, axis=1) * sin_signed)
    return (jnp.concatenate(parts, axis=1),)


def _ep_sigmoid(dots, extras):
    return (jax.nn.sigmoid(dots[0]),)


def _ep_merge(dots, extras):
    y_a, y_b = dots
    s_a, s_b = extras
    return (s_a.astype(F32) * y_a + s_b.astype(F32) * y_b,)


ATTN_HEADS_PER_STEP = 2


def _moba_kernel(q_ref, k_ref, v_ref, o_ref, km_ref):
    tq = MOBA_BLOCK
    nb = N_KV_BLOCKS
    contract_lanes = (((1,), (1,)), ((), ()))
    eye = (lax.broadcasted_iota(jnp.int32, (tq, tq), 0)
           == lax.broadcasted_iota(jnp.int32, (tq, tq), 1)).astype(BF16)
    causal = (lax.broadcasted_iota(jnp.int32, (tq, tq), 1)
              <= lax.broadcasted_iota(jnp.int32, (tq, tq), 0))
    blk_row = lax.broadcasted_iota(jnp.int32, (nb, SEQ), 0)
    past = blk_row < lax.broadcasted_iota(jnp.int32, (nb, SEQ), 1) // MOBA_BLOCK

    for hh in range(ATTN_HEADS_PER_STEP):
        cols = slice(hh * HEAD_DIM, (hh + 1) * HEAD_DIM)
        kf = k_ref[:, cols]
        for n in range(nb):
            km_ref[n:n + 1, :] = jnp.sum(kf[n * MOBA_BLOCK:(n + 1) * MOBA_BLOCK], axis=0,
                                         keepdims=True) * (1.0 / MOBA_BLOCK)
        kb = kf.astype(BF16)
        vb = v_ref[:, cols]
        q = q_ref[:, cols]
        gate = lax.dot_general(km_ref[...], q, contract_lanes,
                               precision=lax.Precision.HIGHEST,
                               preferred_element_type=F32)
        gate = jnp.where(past, gate, -jnp.inf)
        rank = jnp.zeros((nb, SEQ), F32)
        for mth in range(nb):
            gm = gate[mth:mth + 1, :]
            beats = (gm > gate) | ((gm == gate) & (mth < blk_row))
            rank = rank + beats.astype(F32)
        member = ((rank < MOBA_TOPK) & past).astype(BF16)
        member = jnp.concatenate([member, jnp.zeros((LANES - nb, SEQ), BF16)], axis=0)
        qs = (q * (HEAD_DIM ** -0.5)).astype(BF16)

        for t in range(nb):
            rows = slice(t * tq, (t + 1) * tq)
            n_keys = (t + 1) * MOBA_BLOCK
            s = lax.dot_general(qs[rows], kb[:n_keys], contract_lanes,
                                preferred_element_type=F32)
            pieces = []
            if t > 0:
                member_col = lax.dot_general(eye, member[:, rows], contract_lanes,
                                             preferred_element_type=F32)
                for n in range(t):
                    sel = member_col[:, n:n + 1] > 0.5
                    pieces.append(jnp.where(sel, s[:, n * MOBA_BLOCK:(n + 1) * MOBA_BLOCK],
                                            -jnp.inf))
            pieces.append(jnp.where(causal, s[:, t * MOBA_BLOCK:], -jnp.inf))
            masked = pieces[0] if t == 0 else jnp.concatenate(pieces, axis=1)
            m_row = jnp.max(masked, axis=1, keepdims=True)
            p = jnp.exp(masked - m_row)
            l_row = jnp.sum(p, axis=1, keepdims=True)
            acc = jnp.dot(p.astype(BF16), vb[:n_keys], preferred_element_type=F32)
            o_ref[rows, cols] = (acc / l_row).astype(o_ref.dtype)


def _moba_attention(qk, v):
    width = ATTN_HEADS_PER_STEP * HEAD_DIM
    n_groups = N_ATTN_HEADS // ATTN_HEADS_PER_STEP
    return pl.pallas_call(
        _moba_kernel,
        name="moba_attention",
        grid=(BATCH, n_groups),
        in_specs=[
            pl.BlockSpec((SEQ, width), lambda b, g: (b, g)),
            pl.BlockSpec((SEQ, width), lambda b, g: (b, n_groups + g)),
            pl.BlockSpec((SEQ, width), lambda b, g: (b, g)),
        ],
        out_specs=pl.BlockSpec((SEQ, width), lambda b, g: (b, g)),
        out_shape=jax.ShapeDtypeStruct((TOKENS, ATTN_WIDTH), BF16),
        scratch_shapes=[pltpu.VMEM((N_KV_BLOCKS, HEAD_DIM), F32)],
        compiler_params=_params(("parallel", "parallel")),
    )(qk, qk, v)


RG_TILE = 512
RG_CHUNK = 256
CONV_PAD = SUBLANES


def _rglru_kernel(x_ref, xg_ref, cw_ref, cb_ref, wa_ref, ba_ref, wx_ref, bx_ref, lam_ref,
                  o_ref, xp_ref, a_ref, u_ref):
    wt = RG_TILE
    xp_ref[0:CONV_PAD, :] = jnp.zeros((CONV_PAD, wt), F32)
    xp_ref[CONV_PAD:, :] = x_ref[...]

    z = -lam_ref[...]
    softplus = jnp.maximum(z, 0.0) + jnp.log1p(jnp.exp(-jnp.abs(z)))
    cw = cw_ref[...]
    cb = cb_ref[...]
    for c in range(SEQ // RG_CHUNK):
        r0 = c * RG_CHUNK
        y = jnp.broadcast_to(cb, (RG_CHUNK, wt))
        for tap in range(CONV_WIDTH):
            start = r0 + CONV_PAD - (CONV_WIDTH - 1) + tap
            y = y + xp_ref[start:start + RG_CHUNK, :] * cw[tap:tap + 1, :]
        yb = y.astype(BF16)
        for n in range(wt // LRU_BLOCK_WIDTH):
            sl = slice(n * LRU_BLOCK_WIDTH, (n + 1) * LRU_BLOCK_WIDTH)
            ra = jnp.dot(yb[:, sl], wa_ref[n].astype(BF16), preferred_element_type=F32)
            rx = jnp.dot(yb[:, sl], wx_ref[n].astype(BF16), preferred_element_type=F32)
            r = jax.nn.sigmoid(ra + ba_ref[:, sl])
            i = jax.nn.sigmoid(rx + bx_ref[:, sl])
            log_a = -LRU_C * r * softplus[:, sl]
            a_ref[r0:r0 + RG_CHUNK, sl] = jnp.exp(log_a)
            t = jnp.tanh(log_a)
            u_ref[r0:r0 + RG_CHUNK, sl] = y[:, sl] * i * jnp.sqrt(-2.0 * t / (1.0 - t))

    row8 = lax.broadcasted_iota(jnp.int32, (SUBLANES, wt), 0)

    def group(g, h_prev):
        r0 = pl.multiple_of(g * SUBLANES, SUBLANES)
        a = a_ref[pl.ds(r0, SUBLANES), :]
        u = u_ref[pl.ds(r0, SUBLANES), :]
        for d in (1, 2, 4):
            keep = row8 >= d
            a_s = jnp.where(keep, pltpu.roll(a, d, axis=0), 1.0)
            u_s = jnp.where(keep, pltpu.roll(u, d, axis=0), 0.0)
            u = u + a * u_s
            a = a * a_s
        h = u + a * h_prev
        u_ref[pl.ds(r0, SUBLANES), :] = h
        return jnp.broadcast_to(h[SUBLANES - 1:SUBLANES, :], (SUBLANES, wt))

    lax.fori_loop(0, SEQ // SUBLANES, group, jnp.zeros((SUBLANES, wt), F32), unroll=4)

    for c in range(SEQ // RG_CHUNK):
        r0 = c * RG_CHUNK
        gate = jax.nn.gelu(xg_ref[r0:r0 + RG_CHUNK, :])
        o_ref[r0:r0 + RG_CHUNK, :] = (u_ref[r0:r0 + RG_CHUNK, :] * gate).astype(o_ref.dtype)


def _rglru(rec_gate, conv_w, conv_b, w_a, b_a, w_x, b_x, lam):
    wt = RG_TILE
    n_wt = LRU_WIDTH // wt
    nb = wt // LRU_BLOCK_WIDTH
    vec = pl.BlockSpec((1, wt), lambda b, j: (0, j))
    blk = pl.BlockSpec((nb, LRU_BLOCK_WIDTH, LRU_BLOCK_WIDTH), lambda b, j: (j, 0, 0))
    return pl.pallas_call(
        _rglru_kernel,
        name="conv_rglru",
        grid=(BATCH, n_wt),
        in_specs=[
            pl.BlockSpec((SEQ, wt), lambda b, j: (b, j)),
            pl.BlockSpec((SEQ, wt), lambda b, j: (b, j + n_wt)),
            pl.BlockSpec((CONV_WIDTH, wt), lambda b, j: (0, j)),
            vec, blk, vec, blk, vec, vec,
        ],
        out_specs=pl.BlockSpec((SEQ, wt), lambda b, j: (b, j)),
        out_shape=jax.ShapeDtypeStruct((TOKENS, LRU_WIDTH), BF16),
        scratch_shapes=[pltpu.VMEM((SEQ + CONV_PAD, wt), F32),
                        pltpu.VMEM((SEQ, wt), F32),
                        pltpu.VMEM((SEQ, wt), F32)],
        compiler_params=_params(("parallel", "parallel")),
    )(rec_gate, rec_gate, conv_w, conv_b, w_a, b_a, w_x, b_x, lam)


def _rope_tables():
    inv_freq = ROPE_THETA ** (-jnp.arange(0, HEAD_DIM, 2, dtype=F32) / HEAD_DIM)
    ang = jnp.arange(SEQ, dtype=F32)[:, None] * inv_freq[None, :]
    cos, sin = jnp.cos(ang), jnp.sin(ang)
    return jnp.concatenate([cos, cos], axis=1), jnp.concatenate([-sin, sin], axis=1)


def _ffn(xn, w_gate, w_up, w_down):
    (h,) = _mm("ffn_up", [xn], [(w_gate, 0), (w_up, 0)], [(0, 0), (0, 1)], [], _ep_swiglu,
               [BF16], tm=2048, tn=256, n_cols=D_FF, a_buffers=1)
    (f,) = _mm("ffn_down", [h], [(w_down.astype(BF16), 0)], [(0, 0)], [], _ep_identity, [F32],
               tm=512, tn=512, n_cols=D_MODEL)
    return f


def kernel(x, ffn1_pre_g, ffn1_w_gate, ffn1_w_up, ffn1_w_down, ffn1_post_g, mix_pre_g, w_in, conv_w, conv_b, rg_w_a, rg_b_a, rg_w_x, rg_b_x, lru_lambda, w_attn_out, w_rec_out, w_o, mix_post_g, ffn2_pre_g, ffn2_w_gate, ffn2_w_up, ffn2_w_down, ffn2_post_g):
    depth = ffn1_pre_g.shape[0]
    cos_t, sin_t = _rope_tables()
    xf = x.reshape(TOKENS, D_MODEL)
    row = lambda p: p.reshape(1, -1)
    c_qk, c_v, c_rec, c_gate = 0, 2 * ATTN_WIDTH, 3 * ATTN_WIDTH, 3 * ATTN_WIDTH + 2 * LRU_WIDTH
    proj = dict(tm=1024, tn=512)
    for l in range(depth):
        xn = _prenorm(xf, row(ffn1_pre_g[l]))
        f = _ffn(xn, ffn1_w_gate[l], ffn1_w_up[l], ffn1_w_down[l])
        xf, hn = _residual_norm(xf, f, row(ffn1_post_g[l]), row(mix_pre_g[l]),
                                weight=MACARON_WEIGHT)
        (qk,) = _mm("proj_qk_rope", [hn], [(w_in[l], c_qk)], [(0, 0)],
                    [('pos', cos_t), ('pos', sin_t)], _ep_rope, [F32],
                    n_cols=2 * ATTN_WIDTH, **proj)
        (v,) = _mm("proj_v", [hn], [(w_in[l], c_v)], [(0, 0)], [], _ep_identity, [BF16],
                   n_cols=ATTN_WIDTH, **proj)
        (rec_gate,) = _mm("proj_rec", [hn], [(w_in[l], c_rec)], [(0, 0)], [], _ep_identity,
                          [F32], n_cols=2 * LRU_WIDTH, **proj)
        (merge_gates,) = _mm("proj_merge_gates", [hn], [(w_in[l], c_gate)], [(0, 0)], [],
                             _ep_sigmoid, [BF16], n_cols=2 * D_MODEL, **proj)
        attn = _moba_attention(qk, v)
        y_rec = _rglru(rec_gate, conv_w[l], row(conv_b[l]), rg_w_a[l], row(rg_b_a[l]),
                       rg_w_x[l], row(rg_b_x[l]), row(lru_lambda[l]))
        (merged,) = _mm("mixer_out_merge", [attn, y_rec],
                        [(w_attn_out[l], 0), (w_rec_out[l], 0)], [(0, 0), (1, 1)],
                        [('mn', merge_gates, 0), ('mn', merge_gates, D_MODEL)],
                        _ep_merge, [BF16], n_cols=D_MODEL, **proj)
        (mix,) = _mm("proj_o", [merged], [(w_o[l], 0)], [(0, 0)], [], _ep_identity, [F32],
                     n_cols=D_MODEL, **proj)
        xf, xn2 = _residual_norm(xf, mix, row(mix_post_g[l]), row(ffn2_pre_g[l]), weight=1.0)
        f = _ffn(xn2, ffn2_w_gate[l], ffn2_w_up[l], ffn2_w_down[l])
        xf = _residual(xf, f, row(ffn2_post_g[l]), weight=MACARON_WEIGHT)
    return xf.reshape(BATCH, SEQ, D_MODEL)
```

```python
import functools
import math

import jax
import jax.numpy as jnp
from jax import lax
from jax.experimental import pallas as pl
from jax.experimental.pallas import tpu as pltpu

D_MODEL = 4096
BATCH = 4
SEQ = 2048
TOKENS = BATCH * SEQ

N_ATTN_HEADS = 16
HEAD_DIM = 128
ATTN_WIDTH = N_ATTN_HEADS * HEAD_DIM
MOBA_BLOCK = 256
MOBA_TOPK = 3
N_KV_BLOCKS = SEQ // MOBA_BLOCK
ROPE_THETA = 10000.0

LRU_WIDTH = 2048
LRU_BLOCKS = 16
LRU_BLOCK_WIDTH = LRU_WIDTH // LRU_BLOCKS
LRU_C = 8.0
CONV_WIDTH = 4

D_FF = 11008
MACARON_WEIGHT = 0.5
NORM_EPS = 1e-6

V7X_VMEM_LIMIT_BYTES = 56 * 1024 * 1024
LANES = 128
SUBLANES = 8

F32 = jnp.float32
BF16 = jnp.bfloat16


def _params(semantics):
    return pltpu.CompilerParams(dimension_semantics=semantics,
                                vmem_limit_bytes=V7X_VMEM_LIMIT_BYTES)


def _rms(x, g):
    return x * lax.rsqrt(jnp.mean(x * x, axis=-1, keepdims=True) + NORM_EPS) * g


def _sigmoid(x):
    return 0.5 * jnp.tanh(0.5 * x) + 0.5


def _prenorm_kernel(x_ref, g_ref, o_ref):
    o_ref[...] = _rms(x_ref[...], g_ref[...]).astype(o_ref.dtype)


def _prenorm(x, g, *, tm=256):
    m, d = x.shape
    return pl.pallas_call(
        _prenorm_kernel,
        name="prenorm",
        grid=(m // tm,),
        in_specs=[pl.BlockSpec((tm, d), lambda i: (i, 0)),
                  pl.BlockSpec((1, d), lambda i: (0, 0))],
        out_specs=pl.BlockSpec((tm, d), lambda i: (i, 0)),
        out_shape=jax.ShapeDtypeStruct((m, d), BF16),
        compiler_params=_params(("parallel",)),
    )(x, g)


def _residual_norm_kernel(x_ref, f_ref, pg_ref, ng_ref, xo_ref, xn_ref, *, weight):
    xn = x_ref[...] + weight * _rms(f_ref[...].astype(F32), pg_ref[...])
    xo_ref[...] = xn
    xn_ref[...] = _rms(xn, ng_ref[...]).astype(xn_ref.dtype)


def _residual_norm(x, f, post_g, next_g, *, weight, tm=256):
    m, d = x.shape
    row = pl.BlockSpec((tm, d), lambda i: (i, 0))
    vec = pl.BlockSpec((1, d), lambda i: (0, 0))
    return pl.pallas_call(
        functools.partial(_residual_norm_kernel, weight=weight),
        name="residual_norm",
        grid=(m // tm,),
        in_specs=[row, row, vec, vec],
        out_specs=[row, row],
        out_shape=[jax.ShapeDtypeStruct((m, d), F32),
                   jax.ShapeDtypeStruct((m, d), BF16)],
        compiler_params=_params(("parallel",)),
    )(x, f, post_g, next_g)


def _residual_kernel(x_ref, f_ref, pg_ref, xo_ref, *, weight):
    xo_ref[...] = x_ref[...] + weight * _rms(f_ref[...].astype(F32), pg_ref[...])


def _residual(x, f, post_g, *, weight, tm=256):
    m, d = x.shape
    row = pl.BlockSpec((tm, d), lambda i: (i, 0))
    vec = pl.BlockSpec((1, d), lambda i: (0, 0))
    return pl.pallas_call(
        functools.partial(_residual_kernel, weight=weight),
        name="residual_out",
        grid=(m // tm,),
        in_specs=[row, row, vec],
        out_specs=row,
        out_shape=jax.ShapeDtypeStruct((m, d), F32),
        compiler_params=_params(("parallel",)),
    )(x, f, post_g)


def _mm_kernel(*refs, n_a, n_w, n_e, pairs, epilogue, has_side):
    a_refs = refs[:n_a]
    w_refs = refs[n_a:n_a + n_w]
    e_refs = refs[n_a + n_w:n_a + n_w + n_e]
    o_refs = refs[n_a + n_w + n_e:]
    if has_side:
        side_in, side_out = refs[n_a + n_w + n_e], refs[-1]
        o_refs = refs[n_a + n_w + n_e + 1:-1]
        side_out[...] = side_in[...].astype(side_out.dtype)
    w_tiles = [w_ref[...].astype(BF16) for w_ref in w_refs]
    dots = [jnp.dot(a_refs[ai][...], w_tiles[wi], preferred_element_type=F32)
            for ai, wi in pairs]
    outs = epilogue(dots, [e[...] for e in e_refs])
    for o_ref, val in zip(o_refs, outs):
        o_ref[...] = val.astype(o_ref.dtype)


def _mm(name, a_list, w_list, pairs, extras, epilogue, out_dtypes, *, tm, tn, n_cols,
        a_buffers=2, side_cast=None):
    m = a_list[0].shape[0]
    grid = (m // tm, n_cols // tn)
    in_specs, operands = [], []
    a_mode = {} if a_buffers == 2 else {"pipeline_mode": pl.Buffered(a_buffers)}
    for a in a_list:
        in_specs.append(pl.BlockSpec((tm, a.shape[1]), lambda i, j: (i, 0), **a_mode))
        operands.append(a)
    for w, col0 in w_list:
        in_specs.append(pl.BlockSpec((w.shape[0], tn), lambda i, j, c=col0 // tn: (0, j + c)))
        operands.append(w)
    for kind, arr, *rest in extras:
        if kind == 'mn':
            in_specs.append(pl.BlockSpec((tm, tn), lambda i, j, c=rest[0] // tn: (i, j + c)))
        else:
            in_specs.append(pl.BlockSpec((tm, LANES), lambda i, j, p=SEQ // tm: (i % p, 0)))
        operands.append(arr)
    out_specs = [pl.BlockSpec((tm, tn), lambda i, j: (i, j)) for _ in out_dtypes]
    out_shape = [jax.ShapeDtypeStruct((m, n_cols), dt) for dt in out_dtypes]
    if side_cast is not None:
        n_steps = grid[0] * grid[1]
        rows, width = side_cast.shape
        assert rows % n_steps == 0
        slab = pl.BlockSpec((rows // n_steps, width), lambda i, j, nj=grid[1]: (i * nj + j, 0))
        in_specs.append(slab)
        operands.append(side_cast)
        out_specs.append(slab)
        out_shape.append(jax.ShapeDtypeStruct((rows, width), BF16))
    kernel = functools.partial(_mm_kernel, n_a=len(a_list), n_w=len(w_list),
                               n_e=len(extras), pairs=tuple(pairs), epilogue=epilogue,
                               has_side=side_cast is not None)
    outs = pl.pallas_call(
        kernel,
        name=name,
        grid=grid,
        in_specs=in_specs,
        out_specs=out_specs,
        out_shape=out_shape,
        compiler_params=_params(("parallel", "arbitrary")),
    )(*operands)
    return outs


def _ep_swiglu(dots, extras):
    g, u = dots
    return (g * _sigmoid(g) * u,)


def _ep_identity(dots, extras):
    return (dots[0],)


def _ep_rope(dots, extras):
    acc = dots[0]
    cos, sin_signed = extras
    parts = []
    for h in range(acc.shape[1] // HEAD_DIM):
        t = acc[:, h * HEAD_DIM:(h + 1) * HEAD_DIM]
        parts.append(t * cos + pltpu.roll(t, HEAD_DIM // 2, axis=1) * sin_signed)
    return (jnp.concatenate(parts, axis=1),)


def _ep_sigmoid(dots, extras):
    return (_sigmoid(dots[0]),)


def _ep_merge(dots, extras):
    y_a, y_b = dots
    s_a, s_b = extras
    return (s_a.astype(F32) * y_a + s_b.astype(F32) * y_b,)


ATTN_HEADS_PER_STEP = 2
LOG2_E = math.log2(math.e)


def _moba_kernel(q_ref, k_ref, v_ref, o_ref, km_ref):
    tq = MOBA_BLOCK
    nb = N_KV_BLOCKS
    contract_lanes = (((1,), (1,)), ((), ()))
    eye = (lax.broadcasted_iota(jnp.int32, (tq, tq), 0)
           == lax.broadcasted_iota(jnp.int32, (tq, tq), 1)).astype(BF16)
    causal = (lax.broadcasted_iota(jnp.int32, (tq, tq), 1)
              <= lax.broadcasted_iota(jnp.int32, (tq, tq), 0))
    blk_row = lax.broadcasted_iota(jnp.int32, (nb, SEQ), 0)
    past = blk_row < lax.broadcasted_iota(jnp.int32, (nb, SEQ), 1) // MOBA_BLOCK

    heads = []
    for hh in range(ATTN_HEADS_PER_STEP):
        cols = slice(hh * HEAD_DIM, (hh + 1) * HEAD_DIM)
        kf = k_ref[:, cols]
        for n in range(nb):
            km_ref[n:n + 1, :] = jnp.sum(kf[n * MOBA_BLOCK:(n + 1) * MOBA_BLOCK], axis=0,
                                         keepdims=True) * (1.0 / MOBA_BLOCK)
        kb = kf.astype(BF16)
        v_ext = jnp.concatenate([v_ref[:, cols], jnp.ones((SEQ, HEAD_DIM), BF16)], axis=1)
        q = q_ref[:, cols]
        gate = lax.dot_general(km_ref[...], q, contract_lanes,
                               precision=lax.Precision.HIGHEST,
                               preferred_element_type=F32)
        gate = jnp.where(past, gate, -jnp.inf)
        rank = jnp.zeros((nb, SEQ), F32)
        for mth in range(nb):
            gm = gate[mth:mth + 1, :]
            beats = (gm > gate) | ((gm == gate) & (mth < blk_row))
            rank = rank + beats.astype(F32)
        member = ((rank < MOBA_TOPK) & past).astype(BF16)
        member = jnp.concatenate([member, jnp.zeros((LANES - nb, SEQ), BF16)], axis=0)
        qs = (q * (HEAD_DIM ** -0.5 * LOG2_E)).astype(BF16)
        heads.append((cols, kb, v_ext, member, qs))

    def scores(t, hh):
        _, kb, _, _, qs = heads[hh]
        return lax.dot_general(qs[t * tq:(t + 1) * tq], kb[:(t + 1) * MOBA_BLOCK],
                               contract_lanes, preferred_element_type=F32)

    def finish(t, hh, s):
        cols, _, v_ext, member, _ = heads[hh]
        rows = slice(t * tq, (t + 1) * tq)
        pieces = []
        if t > 0:
            member_col = lax.dot_general(eye, member[:, rows], contract_lanes,
                                         preferred_element_type=F32)
            for n in range(t):
                sel = member_col[:, n:n + 1] > 0.5
                pieces.append(jnp.where(sel, s[:, n * MOBA_BLOCK:(n + 1) * MOBA_BLOCK],
                                        -jnp.inf))
        pieces.append(jnp.where(causal, s[:, t * MOBA_BLOCK:], -jnp.inf))
        masked = pieces[0] if t == 0 else jnp.concatenate(pieces, axis=1)
        m_row = jnp.max(masked, axis=1, keepdims=True)
        p = jnp.exp2(masked - m_row).astype(BF16)
        acc = jnp.dot(p, v_ext[:(t + 1) * MOBA_BLOCK], preferred_element_type=F32)
        o_ref[rows, cols] = (acc[:, :HEAD_DIM] / acc[:, HEAD_DIM:HEAD_DIM + 1]).astype(o_ref.dtype)

    items = [(t, hh) for t in range(nb) for hh in range(ATTN_HEADS_PER_STEP)]
    s_cur = scores(*items[0])
    for idx, item in enumerate(items):
        s_next = scores(*items[idx + 1]) if idx + 1 < len(items) else None
        finish(*item, s_cur)
        s_cur = s_next


def _moba_attention(qk, v):
    width = ATTN_HEADS_PER_STEP * HEAD_DIM
    n_groups = N_ATTN_HEADS // ATTN_HEADS_PER_STEP
    return pl.pallas_call(
        _moba_kernel,
        name="moba_attention",
        grid=(BATCH, n_groups),
        in_specs=[
            pl.BlockSpec((SEQ, width), lambda b, g: (b, g)),
            pl.BlockSpec((SEQ, width), lambda b, g: (b, n_groups + g)),
            pl.BlockSpec((SEQ, width), lambda b, g: (b, g)),
        ],
        out_specs=pl.BlockSpec((SEQ, width), lambda b, g: (b, g)),
        out_shape=jax.ShapeDtypeStruct((TOKENS, ATTN_WIDTH), BF16),
        scratch_shapes=[pltpu.VMEM((N_KV_BLOCKS, HEAD_DIM), F32)],
        compiler_params=_params(("parallel", "parallel")),
    )(qk, qk, v)


RG_TILE = 512
RG_CHUNK = 256
CONV_PAD = SUBLANES


def _rglru_kernel(x_ref, xg_ref, cw_ref, cb_ref, wa_ref, ba_ref, wx_ref, bx_ref, lam_ref,
                  o_ref, xp_ref, a_ref, u_ref):
    wt = RG_TILE
    xp_ref[0:CONV_PAD, :] = jnp.zeros((CONV_PAD, wt), F32)
    xp_ref[CONV_PAD:, :] = x_ref[...]

    z = -lam_ref[...]
    softplus = jnp.maximum(z, 0.0) + jnp.log1p(jnp.exp(-jnp.abs(z)))
    cw = cw_ref[...]
    cb = cb_ref[...]
    for c in range(SEQ // RG_CHUNK):
        r0 = c * RG_CHUNK
        y = jnp.broadcast_to(cb, (RG_CHUNK, wt))
        for tap in range(CONV_WIDTH):
            start = r0 + CONV_PAD - (CONV_WIDTH - 1) + tap
            y = y + xp_ref[start:start + RG_CHUNK, :] * cw[tap:tap + 1, :]
        yb = y.astype(BF16)
        for n in range(wt // LRU_BLOCK_WIDTH):
            sl = slice(n * LRU_BLOCK_WIDTH, (n + 1) * LRU_BLOCK_WIDTH)
            ra = jnp.dot(yb[:, sl], wa_ref[n].astype(BF16), preferred_element_type=F32)
            rx = jnp.dot(yb[:, sl], wx_ref[n].astype(BF16), preferred_element_type=F32)
            r = _sigmoid(ra + ba_ref[:, sl])
            i = _sigmoid(rx + bx_ref[:, sl])
            log_a = -LRU_C * r * softplus[:, sl]
            a_ref[r0:r0 + RG_CHUNK, sl] = jnp.exp(log_a)
            t = jnp.tanh(log_a)
            u_ref[r0:r0 + RG_CHUNK, sl] = y[:, sl] * i * jnp.sqrt(-2.0 * t / (1.0 - t))

    row8 = lax.broadcasted_iota(jnp.int32, (SUBLANES, wt), 0)

    def group(g, h_prev):
        r0 = pl.multiple_of(g * SUBLANES, SUBLANES)
        a = a_ref[pl.ds(r0, SUBLANES), :]
        u = u_ref[pl.ds(r0, SUBLANES), :]
        for d in (1, 2, 4):
            keep = row8 >= d
            a_s = jnp.where(keep, pltpu.roll(a, d, axis=0), 1.0)
            u_s = jnp.where(keep, pltpu.roll(u, d, axis=0), 0.0)
            u = u + a * u_s
            a = a * a_s
        h = u + a * h_prev
        u_ref[pl.ds(r0, SUBLANES), :] = h
        return jnp.broadcast_to(h[SUBLANES - 1:SUBLANES, :], (SUBLANES, wt))

    lax.fori_loop(0, SEQ // SUBLANES, group, jnp.zeros((SUBLANES, wt), F32), unroll=4)

    for c in range(SEQ // RG_CHUNK):
        r0 = c * RG_CHUNK
        gate = jax.nn.gelu(xg_ref[r0:r0 + RG_CHUNK, :])
        o_ref[r0:r0 + RG_CHUNK, :] = (u_ref[r0:r0 + RG_CHUNK, :] * gate).astype(o_ref.dtype)


def _rglru(rec_gate, conv_w, conv_b, w_a, b_a, w_x, b_x, lam):
    wt = RG_TILE
    n_wt = LRU_WIDTH // wt
    nb = wt // LRU_BLOCK_WIDTH
    vec = pl.BlockSpec((1, wt), lambda b, j: (0, j))
    blk = pl.BlockSpec((nb, LRU_BLOCK_WIDTH, LRU_BLOCK_WIDTH), lambda b, j: (j, 0, 0))
    return pl.pallas_call(
        _rglru_kernel,
        name="conv_rglru",
        grid=(BATCH, n_wt),
        in_specs=[
            pl.BlockSpec((SEQ, wt), lambda b, j: (b, j)),
            pl.BlockSpec((SEQ, wt), lambda b, j: (b, j + n_wt)),
            pl.BlockSpec((CONV_WIDTH, wt), lambda b, j: (0, j)),
            vec, blk, vec, blk, vec, vec,
        ],
        out_specs=pl.BlockSpec((SEQ, wt), lambda b, j: (b, j)),
        out_shape=jax.ShapeDtypeStruct((TOKENS, LRU_WIDTH), BF16),
        scratch_shapes=[pltpu.VMEM((SEQ + CONV_PAD, wt), F32),
                        pltpu.VMEM((SEQ, wt), F32),
                        pltpu.VMEM((SEQ, wt), F32)],
        compiler_params=_params(("parallel", "parallel")),
    )(rec_gate, rec_gate, conv_w, conv_b, w_a, b_a, w_x, b_x, lam)


def _rope_tables():
    inv_freq = ROPE_THETA ** (-jnp.arange(0, HEAD_DIM, 2, dtype=F32) / HEAD_DIM)
    ang = jnp.arange(SEQ, dtype=F32)[:, None] * inv_freq[None, :]
    cos, sin = jnp.cos(ang), jnp.sin(ang)
    return jnp.concatenate([cos, cos], axis=1), jnp.concatenate([-sin, sin], axis=1)


def _ffn(xn, w_gate, w_up, w_down):
    h, w_down_bf = _mm("ffn_up", [xn], [(w_gate, 0), (w_up, 0)], [(0, 0), (0, 1)], [],
                       _ep_swiglu, [BF16], tm=2048, tn=256, n_cols=D_FF, a_buffers=1,
                       side_cast=w_down)
    (f,) = _mm("ffn_down", [h], [(w_down_bf, 0)], [(0, 0)], [], _ep_identity, [BF16],
               tm=512, tn=512, n_cols=D_MODEL)
    return f


def kernel(x, ffn1_pre_g, ffn1_w_gate, ffn1_w_up, ffn1_w_down, ffn1_post_g, mix_pre_g, w_in, conv_w, conv_b, rg_w_a, rg_b_a, rg_w_x, rg_b_x, lru_lambda, w_attn_out, w_rec_out, w_o, mix_post_g, ffn2_pre_g, ffn2_w_gate, ffn2_w_up, ffn2_w_down, ffn2_post_g):
    depth = ffn1_pre_g.shape[0]
    cos_t, sin_t = _rope_tables()
    xf = x.reshape(TOKENS, D_MODEL)
    row = lambda p: p.reshape(1, -1)
    c_qk, c_v, c_rec, c_gate = 0, 2 * ATTN_WIDTH, 3 * ATTN_WIDTH, 3 * ATTN_WIDTH + 2 * LRU_WIDTH
    proj = dict(tm=2048, tn=512, a_buffers=1)
    for l in range(depth):
        xn = _prenorm(xf, row(ffn1_pre_g[l]))
        f = _ffn(xn, ffn1_w_gate[l], ffn1_w_up[l], ffn1_w_down[l])
        xf, hn = _residual_norm(xf, f, row(ffn1_post_g[l]), row(mix_pre_g[l]),
                                weight=MACARON_WEIGHT)
        (qk,) = _mm("proj_qk_rope", [hn], [(w_in[l], c_qk)], [(0, 0)],
                    [('pos', cos_t), ('pos', sin_t)], _ep_rope, [F32],
                    n_cols=2 * ATTN_WIDTH, **proj)
        (v,) = _mm("proj_v", [hn], [(w_in[l], c_v)], [(0, 0)], [], _ep_identity, [BF16],
                   n_cols=ATTN_WIDTH, **proj)
        (rec_gate,) = _mm("proj_rec", [hn], [(w_in[l], c_rec)], [(0, 0)], [], _ep_identity,
                          [F32], n_cols=2 * LRU_WIDTH, **proj)
        (merge_gates,) = _mm("proj_merge_gates", [hn], [(w_in[l], c_gate)], [(0, 0)], [],
                             _ep_sigmoid, [BF16], n_cols=2 * D_MODEL, **proj)
        attn = _moba_attention(qk, v)
        y_rec = _rglru(rec_gate, conv_w[l], row(conv_b[l]), rg_w_a[l], row(rg_b_a[l]),
                       rg_w_x[l], row(rg_b_x[l]), row(lru_lambda[l]))
        (merged,) = _mm("mixer_out_merge", [attn, y_rec],
                        [(w_attn_out[l], 0), (w_rec_out[l], 0)], [(0, 0), (1, 1)],
                        [('mn', merge_gates, 0), ('mn', merge_gates, D_MODEL)],
                        _ep_merge, [BF16], n_cols=D_MODEL, tm=1024, tn=512)
        (mix,) = _mm("proj_o", [merged], [(w_o[l], 0)], [(0, 0)], [], _ep_identity, [BF16],
                     n_cols=D_MODEL, **proj)
        xf, xn2 = _residual_norm(xf, mix, row(mix_post_g[l]), row(ffn2_pre_g[l]), weight=1.0)
        f = _ffn(xn2, ffn2_w_gate[l], ffn2_w_up[l], ffn2_w_down[l])
        xf = _residual(xf, f, row(ffn2_post_g[l]), weight=MACARON_WEIGHT)
    return xf.reshape(BATCH, SEQ, D_MODEL)
```

```python
import functools
import math

import jax
import jax.numpy as jnp
from jax import lax
from jax.experimental import pallas as pl
from jax.experimental.pallas import tpu as pltpu

D_MODEL = 4096
BATCH = 4
SEQ = 2048
TOKENS = BATCH * SEQ

N_ATTN_HEADS = 16
HEAD_DIM = 128
ATTN_WIDTH = N_ATTN_HEADS * HEAD_DIM
MOBA_BLOCK = 256
MOBA_TOPK = 3
N_KV_BLOCKS = SEQ // MOBA_BLOCK
ROPE_THETA = 10000.0

LRU_WIDTH = 2048
LRU_BLOCKS = 16
LRU_BLOCK_WIDTH = LRU_WIDTH // LRU_BLOCKS
LRU_C = 8.0
CONV_WIDTH = 4

D_FF = 11008
MACARON_WEIGHT = 0.5
NORM_EPS = 1e-6

V7X_VMEM_LIMIT_BYTES = 56 * 1024 * 1024
LANES = 128
SUBLANES = 8

F32 = jnp.float32
BF16 = jnp.bfloat16


def _params(semantics):
    return pltpu.CompilerParams(dimension_semantics=semantics,
                                vmem_limit_bytes=V7X_VMEM_LIMIT_BYTES)


def _rms(x, g):
    return x * lax.rsqrt(jnp.mean(x * x, axis=-1, keepdims=True) + NORM_EPS) * g


def _sigmoid(x):
    return 0.5 * jnp.tanh(0.5 * x) + 0.5


def _prenorm_kernel(x_ref, g_ref, o_ref):
    o_ref[...] = _rms(x_ref[...], g_ref[...]).astype(o_ref.dtype)


def _prenorm(x, g, *, tm=256):
    m, d = x.shape
    return pl.pallas_call(
        _prenorm_kernel,
        name="prenorm",
        grid=(m // tm,),
        in_specs=[pl.BlockSpec((tm, d), lambda i: (i, 0)),
                  pl.BlockSpec((1, d), lambda i: (0, 0))],
        out_specs=pl.BlockSpec((tm, d), lambda i: (i, 0)),
        out_shape=jax.ShapeDtypeStruct((m, d), BF16),
        compiler_params=_params(("parallel",)),
    )(x, g)


def _residual_norm_kernel(x_ref, f_ref, pg_ref, ng_ref, xo_ref, xn_ref, *, weight):
    xn = x_ref[...] + weight * _rms(f_ref[...].astype(F32), pg_ref[...])
    xo_ref[...] = xn
    xn_ref[...] = _rms(xn, ng_ref[...]).astype(xn_ref.dtype)


def _residual_norm(x, f, post_g, next_g, *, weight, tm=256):
    m, d = x.shape
    row = pl.BlockSpec((tm, d), lambda i: (i, 0))
    vec = pl.BlockSpec((1, d), lambda i: (0, 0))
    return pl.pallas_call(
        functools.partial(_residual_norm_kernel, weight=weight),
        name="residual_norm",
        grid=(m // tm,),
        in_specs=[row, row, vec, vec],
        out_specs=[row, row],
        out_shape=[jax.ShapeDtypeStruct((m, d), F32),
                   jax.ShapeDtypeStruct((m, d), BF16)],
        compiler_params=_params(("parallel",)),
    )(x, f, post_g, next_g)


def _residual_kernel(x_ref, f_ref, pg_ref, xo_ref, *, weight):
    xo_ref[...] = x_ref[...] + weight * _rms(f_ref[...].astype(F32), pg_ref[...])


def _residual(x, f, post_g, *, weight, tm=256):
    m, d = x.shape
    row = pl.BlockSpec((tm, d), lambda i: (i, 0))
    vec = pl.BlockSpec((1, d), lambda i: (0, 0))
    return pl.pallas_call(
        functools.partial(_residual_kernel, weight=weight),
        name="residual_out",
        grid=(m // tm,),
        in_specs=[row, row, vec],
        out_specs=row,
        out_shape=jax.ShapeDtypeStruct((m, d), F32),
        compiler_params=_params(("parallel",)),
    )(x, f, post_g)


def _mm_kernel(*refs, n_a, n_w, n_e, pairs, epilogue, has_side, rope_tiles):
    a_refs = refs[:n_a]
    w_refs = refs[n_a:n_a + n_w]
    e_refs = refs[n_a + n_w:n_a + n_w + n_e]
    o_refs = refs[n_a + n_w + n_e:]
    if has_side:
        side_in, side_out = refs[n_a + n_w + n_e], refs[-1]
        o_refs = refs[n_a + n_w + n_e + 1:-1]
        side_out[...] = side_in[...].astype(side_out.dtype)
    w_tiles = [w_ref[...].astype(BF16) for w_ref in w_refs]
    dots = [jnp.dot(a_refs[ai][...], w_tiles[wi], preferred_element_type=F32)
            for ai, wi in pairs]
    if rope_tiles is not None:
        j = pl.program_id(1)

        @pl.when(j < rope_tiles)
        def _():
            o_refs[0][...] = _ep_rope(dots, [e[...] for e in e_refs])[0].astype(o_refs[0].dtype)

        @pl.when(j >= rope_tiles)
        def _():
            o_refs[0][...] = dots[0].astype(o_refs[0].dtype)
        return
    outs = epilogue(dots, [e[...] for e in e_refs])
    for o_ref, val in zip(o_refs, outs):
        o_ref[...] = val.astype(o_ref.dtype)


def _mm(name, a_list, w_list, pairs, extras, epilogue, out_dtypes, *, tm, tn, n_cols,
        a_buffers=2, side_cast=None, rope_cols=None):
    m = a_list[0].shape[0]
    grid = (m // tm, n_cols // tn)
    in_specs, operands = [], []
    a_mode = {} if a_buffers == 2 else {"pipeline_mode": pl.Buffered(a_buffers)}
    for a in a_list:
        in_specs.append(pl.BlockSpec((tm, a.shape[1]), lambda i, j: (i, 0), **a_mode))
        operands.append(a)
    for w, col0 in w_list:
        in_specs.append(pl.BlockSpec((w.shape[0], tn), lambda i, j, c=col0 // tn: (0, j + c)))
        operands.append(w)
    for kind, arr, *rest in extras:
        if kind == 'mn':
            in_specs.append(pl.BlockSpec((tm, tn), lambda i, j, c=rest[0] // tn: (i, j + c)))
        else:
            in_specs.append(pl.BlockSpec((tm, LANES), lambda i, j, p=SEQ // tm: (i % p, 0)))
        operands.append(arr)
    out_specs = [pl.BlockSpec((tm, tn), lambda i, j: (i, j)) for _ in out_dtypes]
    out_shape = [jax.ShapeDtypeStruct((m, n_cols), dt) for dt in out_dtypes]
    if side_cast is not None:
        n_steps = grid[0] * grid[1]
        rows, width = side_cast.shape
        assert rows % n_steps == 0
        slab = pl.BlockSpec((rows // n_steps, width), lambda i, j, nj=grid[1]: (i * nj + j, 0))
        in_specs.append(slab)
        operands.append(side_cast)
        out_specs.append(slab)
        out_shape.append(jax.ShapeDtypeStruct((rows, width), BF16))
    kernel = functools.partial(_mm_kernel, n_a=len(a_list), n_w=len(w_list),
                               n_e=len(extras), pairs=tuple(pairs), epilogue=epilogue,
                               has_side=side_cast is not None,
                               rope_tiles=None if rope_cols is None else rope_cols // tn)
    outs = pl.pallas_call(
        kernel,
        name=name,
        grid=grid,
        in_specs=in_specs,
        out_specs=out_specs,
        out_shape=out_shape,
        compiler_params=_params(("parallel", "arbitrary")),
    )(*operands)
    return outs


def _ep_swiglu(dots, extras):
    g, u = dots
    return (g * _sigmoid(g) * u,)


def _ep_identity(dots, extras):
    return (dots[0],)


def _ep_rope(dots, extras):
    acc = dots[0]
    cos, sin_signed = extras
    parts = []
    for h in range(acc.shape[1] // HEAD_DIM):
        t = acc[:, h * HEAD_DIM:(h + 1) * HEAD_DIM]
        parts.append(t * cos + pltpu.roll(t, HEAD_DIM // 2, axis=1) * sin_signed)
    return (jnp.concatenate(parts, axis=1),)


def _ep_sigmoid(dots, extras):
    return (_sigmoid(dots[0]),)


def _ep_merge(dots, extras):
    y_a, y_b = dots
    s_a, s_b = extras
    return (s_a.astype(F32) * y_a + s_b.astype(F32) * y_b,)


ATTN_HEADS_PER_STEP = 2
LOG2_E = math.log2(math.e)


def _moba_kernel(q_ref, k_ref, v_ref, o_ref, km_ref):
    tq = MOBA_BLOCK
    nb = N_KV_BLOCKS
    contract_lanes = (((1,), (1,)), ((), ()))
    eye = (lax.broadcasted_iota(jnp.int32, (tq, tq), 0)
           == lax.broadcasted_iota(jnp.int32, (tq, tq), 1)).astype(BF16)
    causal = (lax.broadcasted_iota(jnp.int32, (tq, tq), 1)
              <= lax.broadcasted_iota(jnp.int32, (tq, tq), 0))
    blk_row = lax.broadcasted_iota(jnp.int32, (nb, SEQ), 0)
    past = blk_row < lax.broadcasted_iota(jnp.int32, (nb, SEQ), 1) // MOBA_BLOCK

    heads = []
    for hh in range(ATTN_HEADS_PER_STEP):
        cols = slice(hh * HEAD_DIM, (hh + 1) * HEAD_DIM)
        kb = k_ref[:, cols]
        for n in range(nb):
            blk = kb[n * MOBA_BLOCK:(n + 1) * MOBA_BLOCK].astype(F32)
            km_ref[n:n + 1, :] = jnp.sum(blk, axis=0, keepdims=True) * (1.0 / MOBA_BLOCK)
        v_ext = jnp.concatenate([v_ref[:, cols], jnp.ones((SEQ, HEAD_DIM), BF16)], axis=1)
        q = q_ref[:, cols].astype(F32)
        gate = lax.dot_general(km_ref[...], q, contract_lanes,
                               precision=lax.Precision.HIGHEST,
                               preferred_element_type=F32)
        gate = jnp.where(past, gate, -jnp.inf)
        rank = jnp.zeros((nb, SEQ), F32)
        for mth in range(nb):
            gm = gate[mth:mth + 1, :]
            beats = (gm > gate) | ((gm == gate) & (mth < blk_row))
            rank = rank + beats.astype(F32)
        member = ((rank < MOBA_TOPK) & past).astype(BF16)
        member = jnp.concatenate([member, jnp.zeros((LANES - nb, SEQ), BF16)], axis=0)
        qs = (q * (HEAD_DIM ** -0.5 * LOG2_E)).astype(BF16)
        heads.append((cols, kb, v_ext, member, qs))

    def scores(t, hh):
        _, kb, _, _, qs = heads[hh]
        return lax.dot_general(qs[t * tq:(t + 1) * tq], kb[:(t + 1) * MOBA_BLOCK],
                               contract_lanes, preferred_element_type=F32)

    def finish(t, hh, s):
        cols, _, v_ext, member, _ = heads[hh]
        rows = slice(t * tq, (t + 1) * tq)
        pieces = []
        if t > 0:
            member_col = lax.dot_general(eye, member[:, rows], contract_lanes,
                                         preferred_element_type=F32)
            for n in range(t):
                sel = member_col[:, n:n + 1] > 0.5
                pieces.append(jnp.where(sel, s[:, n * MOBA_BLOCK:(n + 1) * MOBA_BLOCK],
                                        -jnp.inf))
        pieces.append(jnp.where(causal, s[:, t * MOBA_BLOCK:], -jnp.inf))
        masked = pieces[0] if t == 0 else jnp.concatenate(pieces, axis=1)
        m_row = jnp.max(masked, axis=1, keepdims=True)
        p = jnp.exp2(masked - m_row).astype(BF16)
        acc = jnp.dot(p, v_ext[:(t + 1) * MOBA_BLOCK], preferred_element_type=F32)
        o_ref[rows, cols] = (acc[:, :HEAD_DIM] / acc[:, HEAD_DIM:HEAD_DIM + 1]).astype(o_ref.dtype)

    items = [(t, hh) for t in range(nb) for hh in range(ATTN_HEADS_PER_STEP)]
    s_cur = scores(*items[0])
    for idx, item in enumerate(items):
        s_next = scores(*items[idx + 1]) if idx + 1 < len(items) else None
        finish(*item, s_cur)
        s_cur = s_next


def _moba_attention(mixer_in):
    width = ATTN_HEADS_PER_STEP * HEAD_DIM
    n_groups = N_ATTN_HEADS // ATTN_HEADS_PER_STEP
    return pl.pallas_call(
        _moba_kernel,
        name="moba_attention",
        grid=(BATCH, n_groups),
        in_specs=[
            pl.BlockSpec((SEQ, width), lambda b, g: (b, g)),
            pl.BlockSpec((SEQ, width), lambda b, g: (b, n_groups + g)),
            pl.BlockSpec((SEQ, width), lambda b, g: (b, 2 * n_groups + g)),
        ],
        out_specs=pl.BlockSpec((SEQ, width), lambda b, g: (b, g)),
        out_shape=jax.ShapeDtypeStruct((TOKENS, ATTN_WIDTH), BF16),
        scratch_shapes=[pltpu.VMEM((N_KV_BLOCKS, HEAD_DIM), F32)],
        compiler_params=_params(("parallel", "parallel")),
    )(mixer_in, mixer_in, mixer_in)


RG_TILE = 512
RG_CHUNK = 256
CONV_PAD = SUBLANES


def _rglru_kernel(x_ref, xg_ref, cw_ref, cb_ref, wa_ref, ba_ref, wx_ref, bx_ref, lam_ref,
                  o_ref, xp_ref, a_ref, u_ref):
    wt = RG_TILE
    xp_ref[0:CONV_PAD, :] = jnp.zeros((CONV_PAD, wt), F32)
    xp_ref[CONV_PAD:, :] = x_ref[...].astype(F32)

    z = -lam_ref[...]
    softplus = jnp.maximum(z, 0.0) + jnp.log1p(jnp.exp(-jnp.abs(z)))
    cw = cw_ref[...]
    cb = cb_ref[...]
    for c in range(SEQ // RG_CHUNK):
        r0 = c * RG_CHUNK
        y = jnp.broadcast_to(cb, (RG_CHUNK, wt))
        for tap in range(CONV_WIDTH):
            start = r0 + CONV_PAD - (CONV_WIDTH - 1) + tap
            y = y + xp_ref[start:start + RG_CHUNK, :] * cw[tap:tap + 1, :]
        yb = y.astype(BF16)
        for n in range(wt // LRU_BLOCK_WIDTH):
            sl = slice(n * LRU_BLOCK_WIDTH, (n + 1) * LRU_BLOCK_WIDTH)
            ra = jnp.dot(yb[:, sl], wa_ref[n].astype(BF16), preferred_element_type=F32)
            rx = jnp.dot(yb[:, sl], wx_ref[n].astype(BF16), preferred_element_type=F32)
            r = _sigmoid(ra + ba_ref[:, sl])
            i = _sigmoid(rx + bx_ref[:, sl])
            log_a = -LRU_C * r * softplus[:, sl]
            a_ref[r0:r0 + RG_CHUNK, sl] = jnp.exp(log_a)
            t = jnp.tanh(log_a)
            u_ref[r0:r0 + RG_CHUNK, sl] = y[:, sl] * i * jnp.sqrt(-2.0 * t / (1.0 - t))

    row8 = lax.broadcasted_iota(jnp.int32, (SUBLANES, wt), 0)

    def group(g, h_prev):
        r0 = pl.multiple_of(g * SUBLANES, SUBLANES)
        a = a_ref[pl.ds(r0, SUBLANES), :]
        u = u_ref[pl.ds(r0, SUBLANES), :]
        for d in (1, 2, 4):
            keep = row8 >= d
            a_s = jnp.where(keep, pltpu.roll(a, d, axis=0), 1.0)
            u_s = jnp.where(keep, pltpu.roll(u, d, axis=0), 0.0)
            u = u + a * u_s
            a = a * a_s
        h = u + a * h_prev
        u_ref[pl.ds(r0, SUBLANES), :] = h
        return jnp.broadcast_to(h[SUBLANES - 1:SUBLANES, :], (SUBLANES, wt))

    lax.fori_loop(0, SEQ // SUBLANES, group, jnp.zeros((SUBLANES, wt), F32), unroll=4)

    for c in range(SEQ // RG_CHUNK):
        r0 = c * RG_CHUNK
        gate = jax.nn.gelu(xg_ref[r0:r0 + RG_CHUNK, :].astype(F32))
        o_ref[r0:r0 + RG_CHUNK, :] = (u_ref[r0:r0 + RG_CHUNK, :] * gate).astype(o_ref.dtype)


def _rglru(mixer_in, col0, conv_w, conv_b, w_a, b_a, w_x, b_x, lam):
    wt = RG_TILE
    n_wt = LRU_WIDTH // wt
    c0 = col0 // wt
    nb = wt // LRU_BLOCK_WIDTH
    vec = pl.BlockSpec((1, wt), lambda b, j: (0, j))
    blk = pl.BlockSpec((nb, LRU_BLOCK_WIDTH, LRU_BLOCK_WIDTH), lambda b, j: (j, 0, 0))
    return pl.pallas_call(
        _rglru_kernel,
        name="conv_rglru",
        grid=(BATCH, n_wt),
        in_specs=[
            pl.BlockSpec((SEQ, wt), lambda b, j: (b, c0 + j)),
            pl.BlockSpec((SEQ, wt), lambda b, j: (b, c0 + n_wt + j)),
            pl.BlockSpec((CONV_WIDTH, wt), lambda b, j: (0, j)),
            vec, blk, vec, blk, vec, vec,
        ],
        out_specs=pl.BlockSpec((SEQ, wt), lambda b, j: (b, j)),
        out_shape=jax.ShapeDtypeStruct((TOKENS, LRU_WIDTH), BF16),
        scratch_shapes=[pltpu.VMEM((SEQ + CONV_PAD, wt), F32),
                        pltpu.VMEM((SEQ, wt), F32),
                        pltpu.VMEM((SEQ, wt), F32)],
        compiler_params=_params(("parallel", "parallel")),
    )(mixer_in, mixer_in, conv_w, conv_b, w_a, b_a, w_x, b_x, lam)


def _rope_tables():
    inv_freq = ROPE_THETA ** (-jnp.arange(0, HEAD_DIM, 2, dtype=F32) / HEAD_DIM)
    ang = jnp.arange(SEQ, dtype=F32)[:, None] * inv_freq[None, :]
    cos, sin = jnp.cos(ang), jnp.sin(ang)
    return jnp.concatenate([cos, cos], axis=1), jnp.concatenate([-sin, sin], axis=1)


def _ffn(xn, w_gate, w_up, w_down):
    h, w_down_bf = _mm("ffn_up", [xn], [(w_gate, 0), (w_up, 0)], [(0, 0), (0, 1)], [],
                       _ep_swiglu, [BF16], tm=2048, tn=256, n_cols=D_FF, a_buffers=1,
                       side_cast=w_down)
    (f,) = _mm("ffn_down", [h], [(w_down_bf, 0)], [(0, 0)], [], _ep_identity, [BF16],
               tm=512, tn=512, n_cols=D_MODEL)
    return f


def kernel(x, ffn1_pre_g, ffn1_w_gate, ffn1_w_up, ffn1_w_down, ffn1_post_g, mix_pre_g, w_in, conv_w, conv_b, rg_w_a, rg_b_a, rg_w_x, rg_b_x, lru_lambda, w_attn_out, w_rec_out, w_o, mix_post_g, ffn2_pre_g, ffn2_w_gate, ffn2_w_up, ffn2_w_down, ffn2_post_g):
    depth = ffn1_pre_g.shape[0]
    cos_t, sin_t = _rope_tables()
    xf = x.reshape(TOKENS, D_MODEL)
    row = lambda p: p.reshape(1, -1)
    c_rec, c_gate = 3 * ATTN_WIDTH, 3 * ATTN_WIDTH + 2 * LRU_WIDTH
    proj = dict(tm=2048, tn=512, a_buffers=1)
    for l in range(depth):
        xn = _prenorm(xf, row(ffn1_pre_g[l]))
        f = _ffn(xn, ffn1_w_gate[l], ffn1_w_up[l], ffn1_w_down[l])
        xf, hn = _residual_norm(xf, f, row(ffn1_post_g[l]), row(mix_pre_g[l]),
                                weight=MACARON_WEIGHT)
        (mixer_in,) = _mm("proj_mixer_in", [hn], [(w_in[l], 0)], [(0, 0)],
                          [('pos', cos_t), ('pos', sin_t)], None, [BF16],
                          n_cols=c_gate, rope_cols=2 * ATTN_WIDTH, **proj)
        (merge_gates,) = _mm("proj_merge_gates", [hn], [(w_in[l], c_gate)], [(0, 0)], [],
                             _ep_sigmoid, [BF16], n_cols=2 * D_MODEL, **proj)
        attn = _moba_attention(mixer_in)
        y_rec = _rglru(mixer_in, c_rec, conv_w[l], row(conv_b[l]), rg_w_a[l], row(rg_b_a[l]),
                       rg_w_x[l], row(rg_b_x[l]), row(lru_lambda[l]))
        (merged,) = _mm("mixer_out_merge", [attn, y_rec],
                        [(w_attn_out[l], 0), (w_rec_out[l], 0)], [(0, 0), (1, 1)],
                        [('mn', merge_gates, 0), ('mn', merge_gates, D_MODEL)],
                        _ep_merge, [BF16], n_cols=D_MODEL, tm=1024, tn=512)
        (mix,) = _mm("proj_o", [merged], [(w_o[l], 0)], [(0, 0)], [], _ep_identity, [BF16],
                     n_cols=D_MODEL, **proj)
        xf, xn2 = _residual_norm(xf, mix, row(mix_post_g[l]), row(ffn2_pre_g[l]), weight=1.0)
        f = _ffn(xn2, ffn2_w_gate[l], ffn2_w_up[l], ffn2_w_down[l])
        xf = _residual(xf, f, row(ffn2_post_g[l]), weight=MACARON_WEIGHT)
    return xf.reshape(BATCH, SEQ, D_MODEL)
```

```python
import functools
import math

import jax
import jax.numpy as jnp
from jax import lax
from jax.experimental import pallas as pl
from jax.experimental.pallas import tpu as pltpu

D_MODEL = 4096
BATCH = 4
SEQ = 2048
TOKENS = BATCH * SEQ

N_ATTN_HEADS = 16
HEAD_DIM = 128
ATTN_WIDTH = N_ATTN_HEADS * HEAD_DIM
MOBA_BLOCK = 256
MOBA_TOPK = 3
N_KV_BLOCKS = SEQ // MOBA_BLOCK
ROPE_THETA = 10000.0

LRU_WIDTH = 2048
LRU_BLOCKS = 16
LRU_BLOCK_WIDTH = LRU_WIDTH // LRU_BLOCKS
LRU_C = 8.0
CONV_WIDTH = 4

D_FF = 11008
MACARON_WEIGHT = 0.5
NORM_EPS = 1e-6

V7X_VMEM_LIMIT_BYTES = 56 * 1024 * 1024
LANES = 128
SUBLANES = 8

F32 = jnp.float32
BF16 = jnp.bfloat16


def _params(semantics):
    return pltpu.CompilerParams(dimension_semantics=semantics,
                                vmem_limit_bytes=V7X_VMEM_LIMIT_BYTES)


def _rms(x, g):
    return x * lax.rsqrt(jnp.mean(x * x, axis=-1, keepdims=True) + NORM_EPS) * g


def _sigmoid(x):
    return 0.5 * jnp.tanh(0.5 * x) + 0.5


def _prenorm_kernel(x_ref, g_ref, o_ref):
    o_ref[...] = _rms(x_ref[...], g_ref[...]).astype(o_ref.dtype)


def _prenorm(x, g, *, tm=256):
    m, d = x.shape
    return pl.pallas_call(
        _prenorm_kernel,
        name="prenorm",
        grid=(m // tm,),
        in_specs=[pl.BlockSpec((tm, d), lambda i: (i, 0)),
                  pl.BlockSpec((1, d), lambda i: (0, 0))],
        out_specs=pl.BlockSpec((tm, d), lambda i: (i, 0)),
        out_shape=jax.ShapeDtypeStruct((m, d), BF16),
        compiler_params=_params(("parallel",)),
    )(x, g)


def _residual_norm_kernel(x_ref, f_ref, pg_ref, ng_ref, xo_ref, xn_ref, *, weight):
    xn = x_ref[...] + weight * _rms(f_ref[...].astype(F32), pg_ref[...])
    xo_ref[...] = xn
    xn_ref[...] = _rms(xn, ng_ref[...]).astype(xn_ref.dtype)


def _residual_norm(x, f, post_g, next_g, *, weight, tm=256):
    m, d = x.shape
    row = pl.BlockSpec((tm, d), lambda i: (i, 0))
    vec = pl.BlockSpec((1, d), lambda i: (0, 0))
    return pl.pallas_call(
        functools.partial(_residual_norm_kernel, weight=weight),
        name="residual_norm",
        grid=(m // tm,),
        in_specs=[row, row, vec, vec],
        out_specs=[row, row],
        out_shape=[jax.ShapeDtypeStruct((m, d), F32),
                   jax.ShapeDtypeStruct((m, d), BF16)],
        compiler_params=_params(("parallel",)),
    )(x, f, post_g, next_g)


def _residual_kernel(x_ref, f_ref, pg_ref, xo_ref, *, weight):
    xo_ref[...] = x_ref[...] + weight * _rms(f_ref[...].astype(F32), pg_ref[...])


def _residual(x, f, post_g, *, weight, tm=256):
    m, d = x.shape
    row = pl.BlockSpec((tm, d), lambda i: (i, 0))
    vec = pl.BlockSpec((1, d), lambda i: (0, 0))
    return pl.pallas_call(
        functools.partial(_residual_kernel, weight=weight),
        name="residual_out",
        grid=(m // tm,),
        in_specs=[row, row, vec],
        out_specs=row,
        out_shape=jax.ShapeDtypeStruct((m, d), F32),
        compiler_params=_params(("parallel",)),
    )(x, f, post_g)


def _mm_kernel(*refs, n_a, n_w, n_e, pairs, epilogue, has_side, rope_tiles, row_parts):
    a_refs = refs[:n_a]
    w_refs = refs[n_a:n_a + n_w]
    e_refs = refs[n_a + n_w:n_a + n_w + n_e]
    o_refs = refs[n_a + n_w + n_e:]
    if has_side:
        side_in, side_out = refs[n_a + n_w + n_e], refs[-1]
        o_refs = refs[n_a + n_w + n_e + 1:-1]
        side_out[...] = side_in[...].astype(side_out.dtype)
    if rope_tiles is not None:
        j = pl.program_id(1)
        (ai, wi), = pairs
        o_ref = o_refs[0]

        @pl.when(j < rope_tiles)
        def _():
            w = w_refs[wi][...].astype(BF16)
            pm = a_refs[ai].shape[0] // 2
            halves = [jnp.dot(a_refs[ai][r * pm:(r + 1) * pm, :], w,
                              preferred_element_type=F32) for r in range(2)]
            for r in range(2):
                tables = [e[r * pm:(r + 1) * pm, :] for e in e_refs]
                o_ref[r * pm:(r + 1) * pm, :] = _ep_rope([halves[r]], tables)[0].astype(o_ref.dtype)

        @pl.when(j >= rope_tiles)
        def _():
            w = w_refs[wi][...].astype(BF16)
            o_ref[...] = jnp.dot(a_refs[ai][...], w, preferred_element_type=F32).astype(o_ref.dtype)
        return
    w_tiles = [w_ref[...].astype(BF16) for w_ref in w_refs]
    if row_parts > 1:
        tm = a_refs[0].shape[0]
        pm = tm // row_parts
        part_dots = [[jnp.dot(a_refs[ai][r * pm:(r + 1) * pm, :], w_tiles[wi],
                              preferred_element_type=F32) for ai, wi in pairs]
                     for r in range(row_parts)]
        for r in range(row_parts):
            outs = epilogue(part_dots[r], [e[r * pm:(r + 1) * pm, :] for e in e_refs])
            for o_ref, val in zip(o_refs, outs):
                o_ref[r * pm:(r + 1) * pm, :] = val.astype(o_ref.dtype)
        return
    dots = [jnp.dot(a_refs[ai][...], w_tiles[wi], preferred_element_type=F32)
            for ai, wi in pairs]
    outs = epilogue(dots, [e[...] for e in e_refs])
    for o_ref, val in zip(o_refs, outs):
        o_ref[...] = val.astype(o_ref.dtype)


def _mm(name, a_list, w_list, pairs, extras, epilogue, out_dtypes, *, tm, tn, n_cols,
        a_buffers=2, side_cast=None, rope_cols=None, row_parts=1):
    m = a_list[0].shape[0]
    grid = (m // tm, n_cols // tn)
    in_specs, operands = [], []
    a_mode = {} if a_buffers == 2 else {"pipeline_mode": pl.Buffered(a_buffers)}
    for a in a_list:
        in_specs.append(pl.BlockSpec((tm, a.shape[1]), lambda i, j: (i, 0), **a_mode))
        operands.append(a)
    for w, col0 in w_list:
        in_specs.append(pl.BlockSpec((w.shape[0], tn), lambda i, j, c=col0 // tn: (0, j + c)))
        operands.append(w)
    for kind, arr, *rest in extras:
        if kind == 'mn':
            in_specs.append(pl.BlockSpec((tm, tn), lambda i, j, c=rest[0] // tn: (i, j + c)))
        else:
            in_specs.append(pl.BlockSpec((tm, LANES), lambda i, j, p=SEQ // tm: (i % p, 0)))
        operands.append(arr)
    out_specs = [pl.BlockSpec((tm, tn), lambda i, j: (i, j)) for _ in out_dtypes]
    out_shape = [jax.ShapeDtypeStruct((m, n_cols), dt) for dt in out_dtypes]
    if side_cast is not None:
        n_steps = grid[0] * grid[1]
        rows, width = side_cast.shape
        assert rows % n_steps == 0
        slab = pl.BlockSpec((rows // n_steps, width), lambda i, j, nj=grid[1]: (i * nj + j, 0))
        in_specs.append(slab)
        operands.append(side_cast)
        out_specs.append(slab)
        out_shape.append(jax.ShapeDtypeStruct((rows, width), BF16))
    kernel = functools.partial(_mm_kernel, n_a=len(a_list), n_w=len(w_list),
                               n_e=len(extras), pairs=tuple(pairs), epilogue=epilogue,
                               has_side=side_cast is not None,
                               rope_tiles=None if rope_cols is None else rope_cols // tn,
                               row_parts=row_parts)
    outs = pl.pallas_call(
        kernel,
        name=name,
        grid=grid,
        in_specs=in_specs,
        out_specs=out_specs,
        out_shape=out_shape,
        compiler_params=_params(("parallel", "arbitrary")),
    )(*operands)
    return outs


def _ep_swiglu(dots, extras):
    g, u = dots
    return (g * _sigmoid(g) * u,)


def _ep_identity(dots, extras):
    return (dots[0],)


def _ep_rope(dots, extras):
    acc = dots[0]
    cos, sin_signed = extras
    parts = []
    for h in range(acc.shape[1] // HEAD_DIM):
        t = acc[:, h * HEAD_DIM:(h + 1) * HEAD_DIM]
        parts.append(t * cos + pltpu.roll(t, HEAD_DIM // 2, axis=1) * sin_signed)
    return (jnp.concatenate(parts, axis=1),)


def _ep_sigmoid(dots, extras):
    return (_sigmoid(dots[0]),)


def _ep_merge(dots, extras):
    y_a, y_b = dots
    s_a, s_b = extras
    return (s_a.astype(F32) * y_a + s_b.astype(F32) * y_b,)


ATTN_HEADS_PER_STEP = 2
LOG2_E = math.log2(math.e)


def _moba_kernel(q_ref, k_ref, v_ref, o_ref, km_ref):
    tq = MOBA_BLOCK
    nb = N_KV_BLOCKS
    contract_lanes = (((1,), (1,)), ((), ()))
    eye = (lax.broadcasted_iota(jnp.int32, (tq, tq), 0)
           == lax.broadcasted_iota(jnp.int32, (tq, tq), 1)).astype(BF16)
    causal = (lax.broadcasted_iota(jnp.int32, (tq, tq), 1)
              <= lax.broadcasted_iota(jnp.int32, (tq, tq), 0))
    blk_row = lax.broadcasted_iota(jnp.int32, (nb, SEQ), 0)
    past = blk_row < lax.broadcasted_iota(jnp.int32, (nb, SEQ), 1) // MOBA_BLOCK

    heads = []
    for hh in range(ATTN_HEADS_PER_STEP):
        cols = slice(hh * HEAD_DIM, (hh + 1) * HEAD_DIM)
        kb = k_ref[:, cols]
        v_ext = jnp.concatenate([v_ref[:, cols], jnp.ones((SEQ, HEAD_DIM), BF16)], axis=1)
        q = q_ref[:, cols].astype(F32)
        qs = (q * (HEAD_DIM ** -0.5 * LOG2_E)).astype(BF16)
        heads.append((cols, kb, v_ext, q, qs))

    def block_members(hh):
        _, kb, _, q, _ = heads[hh]
        for n in range(nb):
            blk = kb[n * MOBA_BLOCK:(n + 1) * MOBA_BLOCK].astype(F32)
            km_ref[n:n + 1, :] = jnp.sum(blk, axis=0, keepdims=True) * (1.0 / MOBA_BLOCK)
        gate = lax.dot_general(km_ref[...], q, contract_lanes,
                               precision=lax.Precision.HIGHEST,
                               preferred_element_type=F32)
        gate = jnp.where(past, gate, -jnp.inf)
        rank = jnp.zeros((nb, SEQ), F32)
        for mth in range(nb):
            gm = gate[mth:mth + 1, :]
            beats = (gm > gate) | ((gm == gate) & (mth < blk_row))
            rank = rank + beats.astype(F32)
        member = ((rank < MOBA_TOPK) & past).astype(BF16)
        return jnp.concatenate([member, jnp.zeros((LANES - nb, SEQ), BF16)], axis=0)

    def scores(t, hh):
        _, kb, _, _, qs = heads[hh]
        return lax.dot_general(qs[t * tq:(t + 1) * tq], kb[:(t + 1) * MOBA_BLOCK],
                               contract_lanes, preferred_element_type=F32)

    def finish(t, hh, s, member):
        cols, _, v_ext, _, _ = heads[hh]
        rows = slice(t * tq, (t + 1) * tq)
        pieces = []
        if t > 0:
            member_col = lax.dot_general(eye, member[:, rows], contract_lanes,
                                         preferred_element_type=F32)
            for n in range(t):
                sel = member_col[:, n:n + 1] > 0.5
                pieces.append(jnp.where(sel, s[:, n * MOBA_BLOCK:(n + 1) * MOBA_BLOCK],
                                        -jnp.inf))
        pieces.append(jnp.where(causal, s[:, t * MOBA_BLOCK:], -jnp.inf))
        masked = pieces[0] if t == 0 else jnp.concatenate(pieces, axis=1)
        m_row = jnp.max(masked, axis=1, keepdims=True)
        p = jnp.exp2(masked - m_row).astype(BF16)
        acc = jnp.dot(p, v_ext[:(t + 1) * MOBA_BLOCK], preferred_element_type=F32)
        o_ref[rows, cols] = (acc[:, :HEAD_DIM] / acc[:, HEAD_DIM:HEAD_DIM + 1]).astype(o_ref.dtype)

    items = [(t, hh) for t in range(nb) for hh in range(ATTN_HEADS_PER_STEP)]
    members = {}
    s_cur = scores(*items[0])
    for idx, (t, hh) in enumerate(items):
        s_next = scores(*items[idx + 1]) if idx + 1 < len(items) else None
        finish(t, hh, s_cur, members.get(hh))
        if t == 0:
            members[hh] = block_members(hh)
        s_cur = s_next


def _moba_attention(mixer_in):
    width = ATTN_HEADS_PER_STEP * HEAD_DIM
    n_groups = N_ATTN_HEADS // ATTN_HEADS_PER_STEP
    return pl.pallas_call(
        _moba_kernel,
        name="moba_attention",
        grid=(BATCH, n_groups),
        in_specs=[
            pl.BlockSpec((SEQ, width), lambda b, g: (b, g)),
            pl.BlockSpec((SEQ, width), lambda b, g: (b, n_groups + g)),
            pl.BlockSpec((SEQ, width), lambda b, g: (b, 2 * n_groups + g)),
        ],
        out_specs=pl.BlockSpec((SEQ, width), lambda b, g: (b, g)),
        out_shape=jax.ShapeDtypeStruct((TOKENS, ATTN_WIDTH), BF16),
        scratch_shapes=[pltpu.VMEM((N_KV_BLOCKS, HEAD_DIM), F32)],
        compiler_params=_params(("parallel", "parallel")),
    )(mixer_in, mixer_in, mixer_in)


RG_TILE = 512
RG_CHUNK = 256
CONV_PAD = SUBLANES


def _rglru_kernel(x_ref, xg_ref, cw_ref, cb_ref, wa_ref, ba_ref, wx_ref, bx_ref, lam_ref,
                  o_ref, xp_ref, a_ref, u_ref):
    wt = RG_TILE
    xp_ref[0:CONV_PAD, :] = jnp.zeros((CONV_PAD, wt), F32)
    xp_ref[CONV_PAD:, :] = x_ref[...].astype(F32)

    z = -lam_ref[...]
    softplus = jnp.maximum(z, 0.0) + jnp.log1p(jnp.exp(-jnp.abs(z)))
    cw = cw_ref[...]
    cb = cb_ref[...]
    for c in range(SEQ // RG_CHUNK):
        r0 = c * RG_CHUNK
        y = jnp.broadcast_to(cb, (RG_CHUNK, wt))
        for tap in range(CONV_WIDTH):
            start = r0 + CONV_PAD - (CONV_WIDTH - 1) + tap
            y = y + xp_ref[start:start + RG_CHUNK, :] * cw[tap:tap + 1, :]
        yb = y.astype(BF16)
        for n in range(wt // LRU_BLOCK_WIDTH):
            sl = slice(n * LRU_BLOCK_WIDTH, (n + 1) * LRU_BLOCK_WIDTH)
            ra = jnp.dot(yb[:, sl], wa_ref[n].astype(BF16), preferred_element_type=F32)
            rx = jnp.dot(yb[:, sl], wx_ref[n].astype(BF16), preferred_element_type=F32)
            r = _sigmoid(ra + ba_ref[:, sl])
            i = _sigmoid(rx + bx_ref[:, sl])
            log_a = -LRU_C * r * softplus[:, sl]
            a_ref[r0:r0 + RG_CHUNK, sl] = jnp.exp(log_a)
            t = jnp.tanh(log_a)
            u_ref[r0:r0 + RG_CHUNK, sl] = y[:, sl] * i * jnp.sqrt(-2.0 * t / (1.0 - t))

    row8 = lax.broadcasted_iota(jnp.int32, (SUBLANES, wt), 0)

    def group(g, h_prev):
        r0 = pl.multiple_of(g * SUBLANES, SUBLANES)
        a = a_ref[pl.ds(r0, SUBLANES), :]
        u = u_ref[pl.ds(r0, SUBLANES), :]
        for d in (1, 2, 4):
            keep = row8 >= d
            a_s = jnp.where(keep, pltpu.roll(a, d, axis=0), 1.0)
            u_s = jnp.where(keep, pltpu.roll(u, d, axis=0), 0.0)
            u = u + a * u_s
            a = a * a_s
        h = u + a * h_prev
        u_ref[pl.ds(r0, SUBLANES), :] = h
        return jnp.broadcast_to(h[SUBLANES - 1:SUBLANES, :], (SUBLANES, wt))

    lax.fori_loop(0, SEQ // SUBLANES, group, jnp.zeros((SUBLANES, wt), F32), unroll=4)

    for c in range(SEQ // RG_CHUNK):
        r0 = c * RG_CHUNK
        gate = jax.nn.gelu(xg_ref[r0:r0 + RG_CHUNK, :].astype(F32))
        o_ref[r0:r0 + RG_CHUNK, :] = (u_ref[r0:r0 + RG_CHUNK, :] * gate).astype(o_ref.dtype)


def _rglru(mixer_in, col0, conv_w, conv_b, w_a, b_a, w_x, b_x, lam):
    wt = RG_TILE
    n_wt = LRU_WIDTH // wt
    c0 = col0 // wt
    nb = wt // LRU_BLOCK_WIDTH
    vec = pl.BlockSpec((1, wt), lambda b, j: (0, j))
    blk = pl.BlockSpec((nb, LRU_BLOCK_WIDTH, LRU_BLOCK_WIDTH), lambda b, j: (j, 0, 0))
    return pl.pallas_call(
        _rglru_kernel,
        name="conv_rglru",
        grid=(BATCH, n_wt),
        in_specs=[
            pl.BlockSpec((SEQ, wt), lambda b, j: (b, c0 + j)),
            pl.BlockSpec((SEQ, wt), lambda b, j: (b, c0 + n_wt + j)),
            pl.BlockSpec((CONV_WIDTH, wt), lambda b, j: (0, j)),
            vec, blk, vec, blk, vec, vec,
        ],
        out_specs=pl.BlockSpec((SEQ, wt), lambda b, j: (b, j)),
        out_shape=jax.ShapeDtypeStruct((TOKENS, LRU_WIDTH), BF16),
        scratch_shapes=[pltpu.VMEM((SEQ + CONV_PAD, wt), F32),
                        pltpu.VMEM((SEQ, wt), F32),
                        pltpu.VMEM((SEQ, wt), F32)],
        compiler_params=_params(("parallel", "parallel")),
    )(mixer_in, mixer_in, conv_w, conv_b, w_a, b_a, w_x, b_x, lam)


def _rope_tables():
    inv_freq = ROPE_THETA ** (-jnp.arange(0, HEAD_DIM, 2, dtype=F32) / HEAD_DIM)
    ang = jnp.arange(SEQ, dtype=F32)[:, None] * inv_freq[None, :]
    cos, sin = jnp.cos(ang), jnp.sin(ang)
    return jnp.concatenate([cos, cos], axis=1), jnp.concatenate([-sin, sin], axis=1)


def _ffn(xn, w_gate, w_up, w_down):
    h, w_down_bf = _mm("ffn_up", [xn], [(w_gate, 0), (w_up, 0)], [(0, 0), (0, 1)], [],
                       _ep_swiglu, [BF16], tm=2048, tn=256, n_cols=D_FF, a_buffers=1,
                       side_cast=w_down, row_parts=2)
    (f,) = _mm("ffn_down", [h], [(w_down_bf, 0)], [(0, 0)], [], _ep_identity, [BF16],
               tm=512, tn=512, n_cols=D_MODEL)
    return f


def kernel(x, ffn1_pre_g, ffn1_w_gate, ffn1_w_up, ffn1_w_down, ffn1_post_g, mix_pre_g, w_in, conv_w, conv_b, rg_w_a, rg_b_a, rg_w_x, rg_b_x, lru_lambda, w_attn_out, w_rec_out, w_o, mix_post_g, ffn2_pre_g, ffn2_w_gate, ffn2_w_up, ffn2_w_down, ffn2_post_g):
    depth = ffn1_pre_g.shape[0]
    cos_t, sin_t = _rope_tables()
    xf = x.reshape(TOKENS, D_MODEL)
    row = lambda p: p.reshape(1, -1)
    c_rec, c_gate = 3 * ATTN_WIDTH, 3 * ATTN_WIDTH + 2 * LRU_WIDTH
    proj = dict(tm=2048, tn=512, a_buffers=1)
    for l in range(depth):
        xn = _prenorm(xf, row(ffn1_pre_g[l]))
        f = _ffn(xn, ffn1_w_gate[l], ffn1_w_up[l], ffn1_w_down[l])
        xf, hn = _residual_norm(xf, f, row(ffn1_post_g[l]), row(mix_pre_g[l]),
                                weight=MACARON_WEIGHT)
        (mixer_in,) = _mm("proj_mixer_in", [hn], [(w_in[l], 0)], [(0, 0)],
                          [('pos', cos_t), ('pos', sin_t)], None, [BF16],
                          n_cols=c_gate, rope_cols=2 * ATTN_WIDTH, **proj)
        (merge_gates,) = _mm("proj_merge_gates", [hn], [(w_in[l], c_gate)], [(0, 0)], [],
                             _ep_sigmoid, [BF16], n_cols=2 * D_MODEL, row_parts=2, **proj)
        attn = _moba_attention(mixer_in)
        y_rec = _rglru(mixer_in, c_rec, conv_w[l], row(conv_b[l]), rg_w_a[l], row(rg_b_a[l]),
                       rg_w_x[l], row(rg_b_x[l]), row(lru_lambda[l]))
        (merged,) = _mm("mixer_out_merge", [attn, y_rec],
                        [(w_attn_out[l], 0), (w_rec_out[l], 0)], [(0, 0), (1, 1)],
                        [('mn', merge_gates, 0), ('mn', merge_gates, D_MODEL)],
                        _ep_merge, [BF16], n_cols=D_MODEL, tm=1024, tn=512)
        (mix,) = _mm("proj_o", [merged], [(w_o[l], 0)], [(0, 0)], [], _ep_identity, [BF16],
                     n_cols=D_MODEL, **proj)
        xf, xn2 = _residual_norm(xf, mix, row(mix_post_g[l]), row(ffn2_pre_g[l]), weight=1.0)
        f = _ffn(xn2, ffn2_w_gate[l], ffn2_w_up[l], ffn2_w_down[l])
        xf = _residual(xf, f, row(ffn2_post_g[l]), weight=MACARON_WEIGHT)
    return xf.reshape(BATCH, SEQ, D_MODEL)
```

```python
import functools
import math

import jax
import jax.numpy as jnp
from jax import lax
from jax.experimental import pallas as pl
from jax.experimental.pallas import tpu as pltpu

D_MODEL = 4096
BATCH = 4
SEQ = 2048
TOKENS = BATCH * SEQ

N_ATTN_HEADS = 16
HEAD_DIM = 128
ATTN_WIDTH = N_ATTN_HEADS * HEAD_DIM
MOBA_BLOCK = 256
MOBA_TOPK = 3
N_KV_BLOCKS = SEQ // MOBA_BLOCK
ROPE_THETA = 10000.0

LRU_WIDTH = 2048
LRU_BLOCKS = 16
LRU_BLOCK_WIDTH = LRU_WIDTH // LRU_BLOCKS
LRU_C = 8.0
CONV_WIDTH = 4

D_FF = 11008
MACARON_WEIGHT = 0.5
NORM_EPS = 1e-6

V7X_VMEM_LIMIT_BYTES = 56 * 1024 * 1024
LANES = 128
SUBLANES = 8

F32 = jnp.float32
BF16 = jnp.bfloat16


def _params(semantics):
    return pltpu.CompilerParams(dimension_semantics=semantics,
                                vmem_limit_bytes=V7X_VMEM_LIMIT_BYTES)


def _rms(x, g):
    return x * lax.rsqrt(jnp.mean(x * x, axis=-1, keepdims=True) + NORM_EPS) * g


def _sigmoid(x):
    return 0.5 * jnp.tanh(0.5 * x) + 0.5


GELU_C0 = math.sqrt(2.0 / math.pi)
GELU_C1 = 0.044715 * GELU_C0


def _gelu_tanh(x):
    half_x = 0.5 * x
    return half_x + half_x * jnp.tanh(x * (GELU_C0 + GELU_C1 * (x * x)))


def _prenorm_kernel(x_ref, g_ref, o_ref):
    o_ref[...] = _rms(x_ref[...], g_ref[...]).astype(o_ref.dtype)


def _prenorm(x, g, *, tm=256):
    m, d = x.shape
    return pl.pallas_call(
        _prenorm_kernel,
        name="prenorm",
        grid=(m // tm,),
        in_specs=[pl.BlockSpec((tm, d), lambda i: (i, 0)),
                  pl.BlockSpec((1, d), lambda i: (0, 0))],
        out_specs=pl.BlockSpec((tm, d), lambda i: (i, 0)),
        out_shape=jax.ShapeDtypeStruct((m, d), BF16),
        compiler_params=_params(("parallel",)),
    )(x, g)


def _residual_norm_kernel(x_ref, f_ref, pg_ref, ng_ref, xo_ref, xn_ref, *, weight):
    xn = x_ref[...] + weight * _rms(f_ref[...].astype(F32), pg_ref[...])
    xo_ref[...] = xn
    xn_ref[...] = _rms(xn, ng_ref[...]).astype(xn_ref.dtype)


def _residual_norm(x, f, post_g, next_g, *, weight, tm=256):
    m, d = x.shape
    row = pl.BlockSpec((tm, d), lambda i: (i, 0))
    vec = pl.BlockSpec((1, d), lambda i: (0, 0))
    return pl.pallas_call(
        functools.partial(_residual_norm_kernel, weight=weight),
        name="residual_norm",
        grid=(m // tm,),
        in_specs=[row, row, vec, vec],
        out_specs=[row, row],
        out_shape=[jax.ShapeDtypeStruct((m, d), F32),
                   jax.ShapeDtypeStruct((m, d), BF16)],
        compiler_params=_params(("parallel",)),
    )(x, f, post_g, next_g)


def _residual_kernel(x_ref, f_ref, pg_ref, xo_ref, *, weight):
    xo_ref[...] = x_ref[...] + weight * _rms(f_ref[...].astype(F32), pg_ref[...])


def _residual(x, f, post_g, *, weight, tm=256):
    m, d = x.shape
    row = pl.BlockSpec((tm, d), lambda i: (i, 0))
    vec = pl.BlockSpec((1, d), lambda i: (0, 0))
    return pl.pallas_call(
        functools.partial(_residual_kernel, weight=weight),
        name="residual_out",
        grid=(m // tm,),
        in_specs=[row, row, vec],
        out_specs=row,
        out_shape=jax.ShapeDtypeStruct((m, d), F32),
        compiler_params=_params(("parallel",)),
    )(x, f, post_g)


def _mm_kernel(*refs, n_a, n_w, n_e, pairs, epilogue, has_side, rope_tiles, row_parts):
    a_refs = refs[:n_a]
    w_refs = refs[n_a:n_a + n_w]
    e_refs = refs[n_a + n_w:n_a + n_w + n_e]
    o_refs = refs[n_a + n_w + n_e:]
    if has_side:
        side_in, side_out = refs[n_a + n_w + n_e], refs[-1]
        o_refs = refs[n_a + n_w + n_e + 1:-1]
        side_out[...] = side_in[...].astype(side_out.dtype)
    if rope_tiles is not None:
        j = pl.program_id(1)
        (ai, wi), = pairs
        o_ref = o_refs[0]

        @pl.when(j < rope_tiles)
        def _():
            w = w_refs[wi][...].astype(BF16)
            pm = a_refs[ai].shape[0] // 2
            halves = [jnp.dot(a_refs[ai][r * pm:(r + 1) * pm, :], w,
                              preferred_element_type=F32) for r in range(2)]
            for r in range(2):
                tables = [e[r * pm:(r + 1) * pm, :] for e in e_refs]
                o_ref[r * pm:(r + 1) * pm, :] = _ep_rope([halves[r]], tables)[0].astype(o_ref.dtype)

        @pl.when(j >= rope_tiles)
        def _():
            w = w_refs[wi][...].astype(BF16)
            o_ref[...] = jnp.dot(a_refs[ai][...], w, preferred_element_type=F32).astype(o_ref.dtype)
        return
    w_tiles = [w_ref[...].astype(BF16) for w_ref in w_refs]
    if row_parts > 1:
        tm = a_refs[0].shape[0]
        pm = tm // row_parts
        part_dots = [[jnp.dot(a_refs[ai][r * pm:(r + 1) * pm, :], w_tiles[wi],
                              preferred_element_type=F32) for ai, wi in pairs]
                     for r in range(row_parts)]
        for r in range(row_parts):
            outs = epilogue(part_dots[r], [e[r * pm:(r + 1) * pm, :] for e in e_refs])
            for o_ref, val in zip(o_refs, outs):
                o_ref[r * pm:(r + 1) * pm, :] = val.astype(o_ref.dtype)
        return
    dots = [jnp.dot(a_refs[ai][...], w_tiles[wi], preferred_element_type=F32)
            for ai, wi in pairs]
    outs = epilogue(dots, [e[...] for e in e_refs])
    for o_ref, val in zip(o_refs, outs):
        o_ref[...] = val.astype(o_ref.dtype)


def _mm(name, a_list, w_list, pairs, extras, epilogue, out_dtypes, *, tm, tn, n_cols,
        a_buffers=2, side_cast=None, rope_cols=None, row_parts=1):
    m = a_list[0].shape[0]
    grid = (m // tm, n_cols // tn)
    in_specs, operands = [], []
    a_mode = {} if a_buffers == 2 else {"pipeline_mode": pl.Buffered(a_buffers)}
    for a in a_list:
        in_specs.append(pl.BlockSpec((tm, a.shape[1]), lambda i, j: (i, 0), **a_mode))
        operands.append(a)
    for w, col0 in w_list:
        in_specs.append(pl.BlockSpec((w.shape[0], tn), lambda i, j, c=col0 // tn: (0, j + c)))
        operands.append(w)
    for kind, arr, *rest in extras:
        if kind == 'mn':
            in_specs.append(pl.BlockSpec((tm, tn), lambda i, j, c=rest[0] // tn: (i, j + c)))
        else:
            in_specs.append(pl.BlockSpec((tm, LANES), lambda i, j, p=SEQ // tm: (i % p, 0)))
        operands.append(arr)
    out_specs = [pl.BlockSpec((tm, tn), lambda i, j: (i, j)) for _ in out_dtypes]
    out_shape = [jax.ShapeDtypeStruct((m, n_cols), dt) for dt in out_dtypes]
    if side_cast is not None:
        n_steps = grid[0] * grid[1]
        rows, width = side_cast.shape
        assert rows % n_steps == 0
        slab = pl.BlockSpec((rows // n_steps, width), lambda i, j, nj=grid[1]: (i * nj + j, 0))
        in_specs.append(slab)
        operands.append(side_cast)
        out_specs.append(slab)
        out_shape.append(jax.ShapeDtypeStruct((rows, width), BF16))
    kernel = functools.partial(_mm_kernel, n_a=len(a_list), n_w=len(w_list),
                               n_e=len(extras), pairs=tuple(pairs), epilogue=epilogue,
                               has_side=side_cast is not None,
                               rope_tiles=None if rope_cols is None else rope_cols // tn,
                               row_parts=row_parts)
    outs = pl.pallas_call(
        kernel,
        name=name,
        grid=grid,
        in_specs=in_specs,
        out_specs=out_specs,
        out_shape=out_shape,
        compiler_params=_params(("parallel", "arbitrary")),
    )(*operands)
    return outs


def _ep_swiglu(dots, extras):
    g, u = dots
    return (g * _sigmoid(g) * u,)


def _ep_identity(dots, extras):
    return (dots[0],)


def _ep_rope(dots, extras):
    acc = dots[0]
    cos, sin_signed = extras
    parts = []
    for h in range(acc.shape[1] // HEAD_DIM):
        t = acc[:, h * HEAD_DIM:(h + 1) * HEAD_DIM]
        parts.append(t * cos + pltpu.roll(t, HEAD_DIM // 2, axis=1) * sin_signed)
    return (jnp.concatenate(parts, axis=1),)


def _ep_sigmoid(dots, extras):
    return (_sigmoid(dots[0]),)


def _ep_merge(dots, extras):
    y_a, y_b = dots
    s_a, s_b = extras
    return (s_a.astype(F32) * y_a + s_b.astype(F32) * y_b,)


ATTN_HEADS_PER_STEP = 2
LOG2_E = math.log2(math.e)


def _moba_kernel(q_ref, k_ref, v_ref, o_ref, km_ref):
    tq = MOBA_BLOCK
    nb = N_KV_BLOCKS
    contract_lanes = (((1,), (1,)), ((), ()))
    eye = (lax.broadcasted_iota(jnp.int32, (tq, tq), 0)
           == lax.broadcasted_iota(jnp.int32, (tq, tq), 1)).astype(BF16)
    causal = (lax.broadcasted_iota(jnp.int32, (tq, tq), 1)
              <= lax.broadcasted_iota(jnp.int32, (tq, tq), 0))
    blk_row = lax.broadcasted_iota(jnp.int32, (nb, SEQ), 0)
    past = blk_row < lax.broadcasted_iota(jnp.int32, (nb, SEQ), 1) // MOBA_BLOCK

    heads = []
    for hh in range(ATTN_HEADS_PER_STEP):
        cols = slice(hh * HEAD_DIM, (hh + 1) * HEAD_DIM)
        kb = k_ref[:, cols]
        v_ext = jnp.concatenate([v_ref[:, cols], jnp.ones((SEQ, HEAD_DIM), BF16)], axis=1)
        q = q_ref[:, cols].astype(F32)
        qs = (q * (HEAD_DIM ** -0.5 * LOG2_E)).astype(BF16)
        heads.append((cols, kb, v_ext, q, qs))

    def block_members(hh):
        _, kb, _, q, _ = heads[hh]
        for n in range(nb):
            blk = kb[n * MOBA_BLOCK:(n + 1) * MOBA_BLOCK].astype(F32)
            km_ref[n:n + 1, :] = jnp.sum(blk, axis=0, keepdims=True) * (1.0 / MOBA_BLOCK)
        gate = lax.dot_general(km_ref[...], q, contract_lanes,
                               precision=lax.Precision.HIGHEST,
                               preferred_element_type=F32)
        gate = jnp.where(past, gate, -jnp.inf)
        rank = jnp.zeros((nb, SEQ), F32)
        for mth in range(nb):
            gm = gate[mth:mth + 1, :]
            beats = (gm > gate) | ((gm == gate) & (mth < blk_row))
            rank = rank + beats.astype(F32)
        member = ((rank < MOBA_TOPK) & past).astype(BF16)
        return jnp.concatenate([member, jnp.zeros((LANES - nb, SEQ), BF16)], axis=0)

    def scores(t, hh):
        _, kb, _, _, qs = heads[hh]
        return lax.dot_general(qs[t * tq:(t + 1) * tq], kb[:(t + 1) * MOBA_BLOCK],
                               contract_lanes, preferred_element_type=F32)

    def finish(t, hh, s, member):
        cols, _, v_ext, _, _ = heads[hh]
        rows = slice(t * tq, (t + 1) * tq)
        pieces = []
        if t > 0:
            member_col = lax.dot_general(eye, member[:, rows], contract_lanes,
                                         preferred_element_type=F32)
            for n in range(t):
                sel = member_col[:, n:n + 1] > 0.5
                pieces.append(jnp.where(sel, s[:, n * MOBA_BLOCK:(n + 1) * MOBA_BLOCK],
                                        -jnp.inf))
        pieces.append(jnp.where(causal, s[:, t * MOBA_BLOCK:], -jnp.inf))
        masked = pieces[0] if t == 0 else jnp.concatenate(pieces, axis=1)
        m_row = jnp.max(masked, axis=1, keepdims=True)
        p = jnp.exp2(masked - m_row).astype(BF16)
        acc = jnp.dot(p, v_ext[:(t + 1) * MOBA_BLOCK], preferred_element_type=F32)
        o_ref[rows, cols] = (acc[:, :HEAD_DIM] / acc[:, HEAD_DIM:HEAD_DIM + 1]).astype(o_ref.dtype)

    items = [(t, hh) for t in range(nb) for hh in range(ATTN_HEADS_PER_STEP)]
    members = {}
    s_cur = scores(*items[0])
    for idx, (t, hh) in enumerate(items):
        s_next = scores(*items[idx + 1]) if idx + 1 < len(items) else None
        finish(t, hh, s_cur, members.get(hh))
        if t == 0:
            members[hh] = block_members(hh)
        s_cur = s_next


def _moba_attention(mixer_in):
    width = ATTN_HEADS_PER_STEP * HEAD_DIM
    n_groups = N_ATTN_HEADS // ATTN_HEADS_PER_STEP
    return pl.pallas_call(
        _moba_kernel,
        name="moba_attention",
        grid=(BATCH, n_groups),
        in_specs=[
            pl.BlockSpec((SEQ, width), lambda b, g: (b, g)),
            pl.BlockSpec((SEQ, width), lambda b, g: (b, n_groups + g)),
            pl.BlockSpec((SEQ, width), lambda b, g: (b, 2 * n_groups + g)),
        ],
        out_specs=pl.BlockSpec((SEQ, width), lambda b, g: (b, g)),
        out_shape=jax.ShapeDtypeStruct((TOKENS, ATTN_WIDTH), BF16),
        scratch_shapes=[pltpu.VMEM((N_KV_BLOCKS, HEAD_DIM), F32)],
        compiler_params=_params(("parallel", "parallel")),
    )(mixer_in, mixer_in, mixer_in)


RG_TILE = 1024
RG_CHUNK = 256
CONV_PAD = SUBLANES


def _rglru_kernel(x_ref, xg_ref, cw_ref, cb_ref, wa_ref, ba_ref, wx_ref, bx_ref, lam_ref,
                  o_ref, xp_ref, a_ref, u_ref):
    wt = RG_TILE
    xp_ref[0:CONV_PAD, :] = jnp.zeros((CONV_PAD, wt), F32)
    xp_ref[CONV_PAD:, :] = x_ref[...].astype(F32)

    z = -lam_ref[...]
    softplus = jnp.maximum(z, 0.0) + jnp.log1p(jnp.exp(-jnp.abs(z)))
    half_rate = (-0.5 * LRU_C) * softplus
    cw = cw_ref[...]
    cb = cb_ref[...]
    for c in range(SEQ // RG_CHUNK):
        r0 = c * RG_CHUNK
        y = jnp.broadcast_to(cb, (RG_CHUNK, wt))
        for tap in range(CONV_WIDTH):
            start = r0 + CONV_PAD - (CONV_WIDTH - 1) + tap
            y = y + xp_ref[start:start + RG_CHUNK, :] * cw[tap:tap + 1, :]
        yb = y.astype(BF16)
        for n in range(wt // LRU_BLOCK_WIDTH):
            sl = slice(n * LRU_BLOCK_WIDTH, (n + 1) * LRU_BLOCK_WIDTH)
            ra = jnp.dot(yb[:, sl], wa_ref[n].astype(BF16), preferred_element_type=F32)
            rx = jnp.dot(yb[:, sl], wx_ref[n].astype(BF16), preferred_element_type=F32)
            i = _sigmoid(rx + bx_ref[:, sl])
            log_a = half_rate[:, sl] * jnp.tanh(0.5 * (ra + ba_ref[:, sl])) + half_rate[:, sl]
            a_ref[r0:r0 + RG_CHUNK, sl] = jnp.exp(log_a)
            t = jnp.tanh(log_a)
            u_ref[r0:r0 + RG_CHUNK, sl] = y[:, sl] * i * jnp.sqrt(-2.0 * t / (1.0 - t))

    row8 = lax.broadcasted_iota(jnp.int32, (SUBLANES, wt), 0)

    def group(g, h_prev):
        r0 = pl.multiple_of(g * SUBLANES, SUBLANES)
        a = a_ref[pl.ds(r0, SUBLANES), :]
        u = u_ref[pl.ds(r0, SUBLANES), :]
        for d in (1, 2, 4):
            keep = row8 >= d
            a_s = jnp.where(keep, pltpu.roll(a, d, axis=0), 1.0)
            u_s = jnp.where(keep, pltpu.roll(u, d, axis=0), 0.0)
            u = u + a * u_s
            a = a * a_s
        h = u + a * h_prev
        u_ref[pl.ds(r0, SUBLANES), :] = h
        return jnp.broadcast_to(h[SUBLANES - 1:SUBLANES, :], (SUBLANES, wt))

    lax.fori_loop(0, SEQ // SUBLANES, group, jnp.zeros((SUBLANES, wt), F32), unroll=4)

    for c in range(SEQ // RG_CHUNK):
        r0 = c * RG_CHUNK
        gate = _gelu_tanh(xg_ref[r0:r0 + RG_CHUNK, :].astype(F32))
        o_ref[r0:r0 + RG_CHUNK, :] = (u_ref[r0:r0 + RG_CHUNK, :] * gate).astype(o_ref.dtype)


def _rglru(mixer_in, col0, conv_w, conv_b, w_a, b_a, w_x, b_x, lam):
    wt = RG_TILE
    n_wt = LRU_WIDTH // wt
    c0 = col0 // wt
    nb = wt // LRU_BLOCK_WIDTH
    vec = pl.BlockSpec((1, wt), lambda b, j: (0, j))
    blk = pl.BlockSpec((nb, LRU_BLOCK_WIDTH, LRU_BLOCK_WIDTH), lambda b, j: (j, 0, 0))
    return pl.pallas_call(
        _rglru_kernel,
        name="conv_rglru",
        grid=(BATCH, n_wt),
        in_specs=[
            pl.BlockSpec((SEQ, wt), lambda b, j: (b, c0 + j)),
            pl.BlockSpec((SEQ, wt), lambda b, j: (b, c0 + n_wt + j)),
            pl.BlockSpec((CONV_WIDTH, wt), lambda b, j: (0, j)),
            vec, blk, vec, blk, vec, vec,
        ],
        out_specs=pl.BlockSpec((SEQ, wt), lambda b, j: (b, j)),
        out_shape=jax.ShapeDtypeStruct((TOKENS, LRU_WIDTH), BF16),
        scratch_shapes=[pltpu.VMEM((SEQ + CONV_PAD, wt), F32),
                        pltpu.VMEM((SEQ, wt), F32),
                        pltpu.VMEM((SEQ, wt), F32)],
        compiler_params=_params(("parallel", "parallel")),
    )(mixer_in, mixer_in, conv_w, conv_b, w_a, b_a, w_x, b_x, lam)


def _rope_tables():
    inv_freq = ROPE_THETA ** (-jnp.arange(0, HEAD_DIM, 2, dtype=F32) / HEAD_DIM)
    ang = jnp.arange(SEQ, dtype=F32)[:, None] * inv_freq[None, :]
    cos, sin = jnp.cos(ang), jnp.sin(ang)
    return jnp.concatenate([cos, cos], axis=1), jnp.concatenate([-sin, sin], axis=1)


def _ffn(xn, w_gate, w_up, w_down):
    h, w_down_bf = _mm("ffn_up", [xn], [(w_gate, 0), (w_up, 0)], [(0, 0), (0, 1)], [],
                       _ep_swiglu, [BF16], tm=2048, tn=256, n_cols=D_FF, a_buffers=1,
                       side_cast=w_down, row_parts=2)
    (f,) = _mm("ffn_down", [h], [(w_down_bf, 0)], [(0, 0)], [], _ep_identity, [BF16],
               tm=512, tn=512, n_cols=D_MODEL)
    return f


def kernel(x, ffn1_pre_g, ffn1_w_gate, ffn1_w_up, ffn1_w_down, ffn1_post_g, mix_pre_g, w_in, conv_w, conv_b, rg_w_a, rg_b_a, rg_w_x, rg_b_x, lru_lambda, w_attn_out, w_rec_out, w_o, mix_post_g, ffn2_pre_g, ffn2_w_gate, ffn2_w_up, ffn2_w_down, ffn2_post_g):
    depth = ffn1_pre_g.shape[0]
    cos_t, sin_t = _rope_tables()
    xf = x.reshape(TOKENS, D_MODEL)
    row = lambda p: p.reshape(1, -1)
    c_rec, c_gate = 3 * ATTN_WIDTH, 3 * ATTN_WIDTH + 2 * LRU_WIDTH
    proj = dict(tm=2048, tn=512, a_buffers=1)
    for l in range(depth):
        xn = _prenorm(xf, row(ffn1_pre_g[l]))
        f = _ffn(xn, ffn1_w_gate[l], ffn1_w_up[l], ffn1_w_down[l])
        xf, hn = _residual_norm(xf, f, row(ffn1_post_g[l]), row(mix_pre_g[l]),
                                weight=MACARON_WEIGHT)
        (mixer_in,) = _mm("proj_mixer_in", [hn], [(w_in[l], 0)], [(0, 0)],
                          [('pos', cos_t), ('pos', sin_t)], None, [BF16],
                          n_cols=c_gate, rope_cols=2 * ATTN_WIDTH, **proj)
        (merge_gates,) = _mm("proj_merge_gates", [hn], [(w_in[l], c_gate)], [(0, 0)], [],
                             _ep_sigmoid, [BF16], n_cols=2 * D_MODEL, row_parts=2, **proj)
        attn = _moba_attention(mixer_in)
        y_rec = _rglru(mixer_in, c_rec, conv_w[l], row(conv_b[l]), rg_w_a[l], row(rg_b_a[l]),
                       rg_w_x[l], row(rg_b_x[l]), row(lru_lambda[l]))
        (merged,) = _mm("mixer_out_merge", [attn, y_rec],
                        [(w_attn_out[l], 0), (w_rec_out[l], 0)], [(0, 0), (1, 1)],
                        [('mn', merge_gates, 0), ('mn', merge_gates, D_MODEL)],
                        _ep_merge, [BF16], n_cols=D_MODEL, tm=1024, tn=512)
        (mix,) = _mm("proj_o", [merged], [(w_o[l], 0)], [(0, 0)], [], _ep_identity, [BF16],
                     n_cols=D_MODEL, **proj)
        xf, xn2 = _residual_norm(xf, mix, row(mix_post_g[l]), row(ffn2_pre_g[l]), weight=1.0)
        f = _ffn(xn2, ffn2_w_gate[l], ffn2_w_up[l], ffn2_w_down[l])
        xf = _residual(xf, f, row(ffn2_post_g[l]), weight=MACARON_WEIGHT)
    return xf.reshape(BATCH, SEQ, D_MODEL)
```

```python
import functools
import math

import jax
import jax.numpy as jnp
from jax import lax
from jax.experimental import pallas as pl
from jax.experimental.pallas import tpu as pltpu

D_MODEL = 4096
BATCH = 4
SEQ = 2048
TOKENS = BATCH * SEQ

N_ATTN_HEADS = 16
HEAD_DIM = 128
ATTN_WIDTH = N_ATTN_HEADS * HEAD_DIM
MOBA_BLOCK = 256
MOBA_TOPK = 3
N_KV_BLOCKS = SEQ // MOBA_BLOCK
ROPE_THETA = 10000.0

LRU_WIDTH = 2048
LRU_BLOCKS = 16
LRU_BLOCK_WIDTH = LRU_WIDTH // LRU_BLOCKS
LRU_C = 8.0
CONV_WIDTH = 4

D_FF = 11008
MACARON_WEIGHT = 0.5
NORM_EPS = 1e-6

V7X_VMEM_LIMIT_BYTES = 56 * 1024 * 1024
LANES = 128
SUBLANES = 8

F32 = jnp.float32
BF16 = jnp.bfloat16


def _params(semantics):
    return pltpu.CompilerParams(dimension_semantics=semantics,
                                vmem_limit_bytes=V7X_VMEM_LIMIT_BYTES)


def _rms(x, g):
    return x * lax.rsqrt(jnp.mean(x * x, axis=-1, keepdims=True) + NORM_EPS) * g


def _sigmoid(x):
    return 0.5 * jnp.tanh(0.5 * x) + 0.5


GELU_C0 = math.sqrt(2.0 / math.pi)
GELU_C1 = 0.044715 * GELU_C0


def _gelu_tanh(x):
    half_x = 0.5 * x
    return half_x + half_x * jnp.tanh(x * (GELU_C0 + GELU_C1 * (x * x)))


RING_DEPTH = 3


def _ring_copy(src, buf, sem, blk, slot):
    tm = buf.shape[1]
    rows = pl.ds(pl.multiple_of(blk * tm, tm), tm)
    return pltpu.make_async_copy(src.at[rows, :], buf.at[slot], sem.at[slot])


def _ring_fetch(srcs, bufs, sems):
    s = pl.program_id(0)
    n = pl.num_programs(0)

    @pl.when(s == 0)
    def _():
        for blk in range(RING_DEPTH - 1):
            for src, buf, sem in zip(srcs, bufs, sems):
                _ring_copy(src, buf, sem, blk, blk).start()

    ahead = s + (RING_DEPTH - 1)

    @pl.when(ahead < n)
    def _():
        for src, buf, sem in zip(srcs, bufs, sems):
            _ring_copy(src, buf, sem, ahead, ahead % RING_DEPTH).start()

    slot = s % RING_DEPTH
    for src, buf, sem in zip(srcs, bufs, sems):
        _ring_copy(src, buf, sem, s, slot).wait()
    return slot


def _ring_scratch(tm, d, dtypes):
    bufs = [pltpu.VMEM((RING_DEPTH, tm, d), dt) for dt in dtypes]
    return bufs + [pltpu.SemaphoreType.DMA((RING_DEPTH,)) for _ in dtypes]


def _prenorm_kernel(x_hbm, g_ref, o_ref, xbuf, xsem):
    slot = _ring_fetch([x_hbm], [xbuf], [xsem])
    o_ref[...] = _rms(xbuf[slot], g_ref[...]).astype(o_ref.dtype)


def _prenorm(x, g, *, tm=256):
    m, d = x.shape
    assert m // tm >= RING_DEPTH - 1
    return pl.pallas_call(
        _prenorm_kernel,
        name="prenorm",
        grid=(m // tm,),
        in_specs=[pl.BlockSpec(memory_space=pl.ANY),
                  pl.BlockSpec((1, d), lambda i: (0, 0))],
        out_specs=pl.BlockSpec((tm, d), lambda i: (i, 0)),
        out_shape=jax.ShapeDtypeStruct((m, d), BF16),
        scratch_shapes=_ring_scratch(tm, d, [x.dtype]),
        compiler_params=_params(("arbitrary",)),
    )(x, g)


def _residual_norm_kernel(x_hbm, f_hbm, pg_ref, ng_ref, xo_ref, xn_ref, xbuf, fbuf, xsem, fsem,
                          *, weight):
    slot = _ring_fetch([x_hbm, f_hbm], [xbuf, fbuf], [xsem, fsem])
    xn = xbuf[slot] + weight * _rms(fbuf[slot].astype(F32), pg_ref[...])
    xo_ref[...] = xn
    xn_ref[...] = _rms(xn, ng_ref[...]).astype(xn_ref.dtype)


def _residual_norm(x, f, post_g, next_g, *, weight, tm=256):
    m, d = x.shape
    assert m // tm >= RING_DEPTH - 1
    row = pl.BlockSpec((tm, d), lambda i: (i, 0))
    vec = pl.BlockSpec((1, d), lambda i: (0, 0))
    hbm = pl.BlockSpec(memory_space=pl.ANY)
    return pl.pallas_call(
        functools.partial(_residual_norm_kernel, weight=weight),
        name="residual_norm",
        grid=(m // tm,),
        in_specs=[hbm, hbm, vec, vec],
        out_specs=[row, row],
        out_shape=[jax.ShapeDtypeStruct((m, d), F32),
                   jax.ShapeDtypeStruct((m, d), BF16)],
        scratch_shapes=_ring_scratch(tm, d, [x.dtype, f.dtype]),
        compiler_params=_params(("arbitrary",)),
    )(x, f, post_g, next_g)


def _residual_kernel(x_hbm, f_hbm, pg_ref, xo_ref, xbuf, fbuf, xsem, fsem, *, weight):
    slot = _ring_fetch([x_hbm, f_hbm], [xbuf, fbuf], [xsem, fsem])
    xo_ref[...] = xbuf[slot] + weight * _rms(fbuf[slot].astype(F32), pg_ref[...])


def _residual(x, f, post_g, *, weight, tm=256):
    m, d = x.shape
    assert m // tm >= RING_DEPTH - 1
    row = pl.BlockSpec((tm, d), lambda i: (i, 0))
    vec = pl.BlockSpec((1, d), lambda i: (0, 0))
    hbm = pl.BlockSpec(memory_space=pl.ANY)
    return pl.pallas_call(
        functools.partial(_residual_kernel, weight=weight),
        name="residual_out",
        grid=(m // tm,),
        in_specs=[hbm, hbm, vec],
        out_specs=row,
        out_shape=jax.ShapeDtypeStruct((m, d), F32),
        scratch_shapes=_ring_scratch(tm, d, [x.dtype, f.dtype]),
        compiler_params=_params(("arbitrary",)),
    )(x, f, post_g)


def _mm_kernel(*refs, n_a, n_w, n_e, pairs, epilogue, has_side, rope_tiles, row_parts):
    a_refs = refs[:n_a]
    w_refs = refs[n_a:n_a + n_w]
    e_refs = refs[n_a + n_w:n_a + n_w + n_e]
    o_refs = refs[n_a + n_w + n_e:]
    if has_side:
        side_in, side_out = refs[n_a + n_w + n_e], refs[-1]
        o_refs = refs[n_a + n_w + n_e + 1:-1]
        side_out[...] = side_in[...].astype(side_out.dtype)
    if rope_tiles is not None:
        j = pl.program_id(1)
        (ai, wi), = pairs
        o_ref = o_refs[0]

        @pl.when(j < rope_tiles)
        def _():
            w = w_refs[wi][...].astype(BF16)
            pm = a_refs[ai].shape[0] // 2
            halves = [jnp.dot(a_refs[ai][r * pm:(r + 1) * pm, :], w,
                              preferred_element_type=F32) for r in range(2)]
            for r in range(2):
                tables = [e[r * pm:(r + 1) * pm, :] for e in e_refs]
                o_ref[r * pm:(r + 1) * pm, :] = _ep_rope([halves[r]], tables)[0].astype(o_ref.dtype)

        @pl.when(j >= rope_tiles)
        def _():
            w = w_refs[wi][...].astype(BF16)
            o_ref[...] = jnp.dot(a_refs[ai][...], w, preferred_element_type=F32).astype(o_ref.dtype)
        return
    w_tiles = [w_ref[...].astype(BF16) for w_ref in w_refs]
    if row_parts > 1:
        tm = a_refs[0].shape[0]
        pm = tm // row_parts
        part_dots = [[jnp.dot(a_refs[ai][r * pm:(r + 1) * pm, :], w_tiles[wi],
                              preferred_element_type=F32) for ai, wi in pairs]
                     for r in range(row_parts)]
        for r in range(row_parts):
            outs = epilogue(part_dots[r], [e[r * pm:(r + 1) * pm, :] for e in e_refs])
            for o_ref, val in zip(o_refs, outs):
                o_ref[r * pm:(r + 1) * pm, :] = val.astype(o_ref.dtype)
        return
    dots = [jnp.dot(a_refs[ai][...], w_tiles[wi], preferred_element_type=F32)
            for ai, wi in pairs]
    outs = epilogue(dots, [e[...] for e in e_refs])
    for o_ref, val in zip(o_refs, outs):
        o_ref[...] = val.astype(o_ref.dtype)


def _mm(name, a_list, w_list, pairs, extras, epilogue, out_dtypes, *, tm, tn, n_cols,
        a_buffers=2, side_cast=None, rope_cols=None, row_parts=1):
    m = a_list[0].shape[0]
    grid = (m // tm, n_cols // tn)
    in_specs, operands = [], []
    a_mode = {} if a_buffers == 2 else {"pipeline_mode": pl.Buffered(a_buffers)}
    for a in a_list:
        in_specs.append(pl.BlockSpec((tm, a.shape[1]), lambda i, j: (i, 0), **a_mode))
        operands.append(a)
    for w, col0 in w_list:
        in_specs.append(pl.BlockSpec((w.shape[0], tn), lambda i, j, c=col0 // tn: (0, j + c)))
        operands.append(w)
    for kind, arr, *rest in extras:
        if kind == 'mn':
            in_specs.append(pl.BlockSpec((tm, tn), lambda i, j, c=rest[0] // tn: (i, j + c)))
        else:
            in_specs.append(pl.BlockSpec((tm, LANES), lambda i, j, p=SEQ // tm: (i % p, 0)))
        operands.append(arr)
    out_specs = [pl.BlockSpec((tm, tn), lambda i, j: (i, j)) for _ in out_dtypes]
    out_shape = [jax.ShapeDtypeStruct((m, n_cols), dt) for dt in out_dtypes]
    if side_cast is not None:
        n_steps = grid[0] * grid[1]
        rows, width = side_cast.shape
        assert rows % n_steps == 0
        slab = pl.BlockSpec((rows // n_steps, width), lambda i, j, nj=grid[1]: (i * nj + j, 0))
        in_specs.append(slab)
        operands.append(side_cast)
        out_specs.append(slab)
        out_shape.append(jax.ShapeDtypeStruct((rows, width), BF16))
    kernel = functools.partial(_mm_kernel, n_a=len(a_list), n_w=len(w_list),
                               n_e=len(extras), pairs=tuple(pairs), epilogue=epilogue,
                               has_side=side_cast is not None,
                               rope_tiles=None if rope_cols is None else rope_cols // tn,
                               row_parts=row_parts)
    outs = pl.pallas_call(
        kernel,
        name=name,
        grid=grid,
        in_specs=in_specs,
        out_specs=out_specs,
        out_shape=out_shape,
        compiler_params=_params(("parallel", "arbitrary")),
    )(*operands)
    return outs


def _ep_swiglu(dots, extras):
    g, u = dots
    return (g * _sigmoid(g) * u,)


def _ep_identity(dots, extras):
    return (dots[0],)


def _ep_rope(dots, extras):
    acc = dots[0]
    cos, sin_signed = extras
    parts = []
    for h in range(acc.shape[1] // HEAD_DIM):
        t = acc[:, h * HEAD_DIM:(h + 1) * HEAD_DIM]
        parts.append(t * cos + pltpu.roll(t, HEAD_DIM // 2, axis=1) * sin_signed)
    return (jnp.concatenate(parts, axis=1),)


def _ep_sigmoid(dots, extras):
    return (_sigmoid(dots[0]),)


def _ep_merge(dots, extras):
    y_a, y_b = dots
    s_a, s_b = extras
    return (s_a.astype(F32) * y_a + s_b.astype(F32) * y_b,)


ATTN_HEADS_PER_STEP = 2
LOG2_E = math.log2(math.e)


def _moba_kernel(q_ref, k_ref, v_ref, o_ref, km_ref):
    tq = MOBA_BLOCK
    nb = N_KV_BLOCKS
    contract_lanes = (((1,), (1,)), ((), ()))
    eye = (lax.broadcasted_iota(jnp.int32, (tq, tq), 0)
           == lax.broadcasted_iota(jnp.int32, (tq, tq), 1)).astype(BF16)
    causal = (lax.broadcasted_iota(jnp.int32, (tq, tq), 1)
              <= lax.broadcasted_iota(jnp.int32, (tq, tq), 0))
    blk_row = lax.broadcasted_iota(jnp.int32, (nb, SEQ), 0)
    past = blk_row < lax.broadcasted_iota(jnp.int32, (nb, SEQ), 1) // MOBA_BLOCK

    heads = []
    for hh in range(ATTN_HEADS_PER_STEP):
        cols = slice(hh * HEAD_DIM, (hh + 1) * HEAD_DIM)
        kb = k_ref[:, cols]
        v_ext = jnp.concatenate([v_ref[:, cols], jnp.ones((SEQ, HEAD_DIM), BF16)], axis=1)
        q = q_ref[:, cols].astype(F32)
        qs = (q * (HEAD_DIM ** -0.5 * LOG2_E)).astype(BF16)
        heads.append((cols, kb, v_ext, q, qs))

    def block_members(hh):
        _, kb, _, q, _ = heads[hh]
        for n in range(nb):
            blk = kb[n * MOBA_BLOCK:(n + 1) * MOBA_BLOCK].astype(F32)
            km_ref[n:n + 1, :] = jnp.sum(blk, axis=0, keepdims=True) * (1.0 / MOBA_BLOCK)
        gate = lax.dot_general(km_ref[...], q, contract_lanes,
                               precision=lax.Precision.HIGHEST,
                               preferred_element_type=F32)
        gate = jnp.where(past, gate, -jnp.inf)
        rank = jnp.zeros((nb, SEQ), F32)
        for mth in range(nb):
            gm = gate[mth:mth + 1, :]
            beats = (gm > gate) | ((gm == gate) & (mth < blk_row))
            rank = rank + beats.astype(F32)
        member = ((rank < MOBA_TOPK) & past).astype(BF16)
        return jnp.concatenate([member, jnp.zeros((LANES - nb, SEQ), BF16)], axis=0)

    def scores(t, hh):
        _, kb, _, _, qs = heads[hh]
        return lax.dot_general(qs[t * tq:(t + 1) * tq], kb[:(t + 1) * MOBA_BLOCK],
                               contract_lanes, preferred_element_type=F32)

    def finish(t, hh, s, member):
        cols, _, v_ext, _, _ = heads[hh]
        rows = slice(t * tq, (t + 1) * tq)
        pieces = []
        if t > 0:
            member_col = lax.dot_general(eye, member[:, rows], contract_lanes,
                                         preferred_element_type=F32)
            for n in range(t):
                sel = member_col[:, n:n + 1] > 0.5
                pieces.append(jnp.where(sel, s[:, n * MOBA_BLOCK:(n + 1) * MOBA_BLOCK],
                                        -jnp.inf))
        pieces.append(jnp.where(causal, s[:, t * MOBA_BLOCK:], -jnp.inf))
        masked = pieces[0] if t == 0 else jnp.concatenate(pieces, axis=1)
        m_row = jnp.max(masked, axis=1, keepdims=True)
        p = jnp.exp2(masked - m_row).astype(BF16)
        acc = jnp.dot(p, v_ext[:(t + 1) * MOBA_BLOCK], preferred_element_type=F32)
        o_ref[rows, cols] = (acc[:, :HEAD_DIM] / acc[:, HEAD_DIM:HEAD_DIM + 1]).astype(o_ref.dtype)

    items = [(t, hh) for t in range(nb) for hh in range(ATTN_HEADS_PER_STEP)]
    members = {}
    s_cur = scores(*items[0])
    for idx, (t, hh) in enumerate(items):
        s_next = scores(*items[idx + 1]) if idx + 1 < len(items) else None
        finish(t, hh, s_cur, members.get(hh))
        if t == 0:
            members[hh] = block_members(hh)
        s_cur = s_next


def _moba_attention(mixer_in):
    width = ATTN_HEADS_PER_STEP * HEAD_DIM
    n_groups = N_ATTN_HEADS // ATTN_HEADS_PER_STEP
    return pl.pallas_call(
        _moba_kernel,
        name="moba_attention",
        grid=(BATCH, n_groups),
        in_specs=[
            pl.BlockSpec((SEQ, width), lambda b, g: (b, g)),
            pl.BlockSpec((SEQ, width), lambda b, g: (b, n_groups + g)),
            pl.BlockSpec((SEQ, width), lambda b, g: (b, 2 * n_groups + g)),
        ],
        out_specs=pl.BlockSpec((SEQ, width), lambda b, g: (b, g)),
        out_shape=jax.ShapeDtypeStruct((TOKENS, ATTN_WIDTH), BF16),
        scratch_shapes=[pltpu.VMEM((N_KV_BLOCKS, HEAD_DIM), F32)],
        compiler_params=_params(("parallel", "parallel")),
    )(mixer_in, mixer_in, mixer_in)


RG_TILE = 1024
RG_CHUNK = 256
CONV_PAD = SUBLANES


def _rglru_kernel(x_ref, xg_ref, cw_ref, cb_ref, wa_ref, ba_ref, wx_ref, bx_ref, lam_ref,
                  o_ref, xp_ref, a_ref, u_ref):
    wt = RG_TILE
    xp_ref[0:CONV_PAD, :] = jnp.zeros((CONV_PAD, wt), F32)
    xp_ref[CONV_PAD:, :] = x_ref[...].astype(F32)

    z = -lam_ref[...]
    softplus = jnp.maximum(z, 0.0) + jnp.log1p(jnp.exp(-jnp.abs(z)))
    half_rate = (-0.5 * LRU_C) * softplus
    cw = cw_ref[...]
    cb = cb_ref[...]
    for c in range(SEQ // RG_CHUNK):
        r0 = c * RG_CHUNK
        y = jnp.broadcast_to(cb, (RG_CHUNK, wt))
        for tap in range(CONV_WIDTH):
            start = r0 + CONV_PAD - (CONV_WIDTH - 1) + tap
            y = y + xp_ref[start:start + RG_CHUNK, :] * cw[tap:tap + 1, :]
        yb = y.astype(BF16)
        for n in range(wt // LRU_BLOCK_WIDTH):
            sl = slice(n * LRU_BLOCK_WIDTH, (n + 1) * LRU_BLOCK_WIDTH)
            ra = jnp.dot(yb[:, sl], wa_ref[n].astype(BF16), preferred_element_type=F32)
            rx = jnp.dot(yb[:, sl], wx_ref[n].astype(BF16), preferred_element_type=F32)
            i = _sigmoid(rx + bx_ref[:, sl])
            log_a = half_rate[:, sl] * jnp.tanh(0.5 * (ra + ba_ref[:, sl])) + half_rate[:, sl]
            a_ref[r0:r0 + RG_CHUNK, sl] = jnp.exp(log_a)
            t = jnp.tanh(log_a)
            u_ref[r0:r0 + RG_CHUNK, sl] = y[:, sl] * i * jnp.sqrt(-2.0 * t / (1.0 - t))

    row8 = lax.broadcasted_iota(jnp.int32, (SUBLANES, wt), 0)

    def group(g, h_prev):
        r0 = pl.multiple_of(g * SUBLANES, SUBLANES)
        a = a_ref[pl.ds(r0, SUBLANES), :]
        u = u_ref[pl.ds(r0, SUBLANES), :]
        for d in (1, 2, 4):
            keep = row8 >= d
            a_s = jnp.where(keep, pltpu.roll(a, d, axis=0), 1.0)
            u_s = jnp.where(keep, pltpu.roll(u, d, axis=0), 0.0)
            u = u + a * u_s
            a = a * a_s
        h = u + a * h_prev
        u_ref[pl.ds(r0, SUBLANES), :] = h
        return jnp.broadcast_to(h[SUBLANES - 1:SUBLANES, :], (SUBLANES, wt))

    lax.fori_loop(0, SEQ // SUBLANES, group, jnp.zeros((SUBLANES, wt), F32), unroll=4)

    for c in range(SEQ // RG_CHUNK):
        r0 = c * RG_CHUNK
        gate = _gelu_tanh(xg_ref[r0:r0 + RG_CHUNK, :].astype(F32))
        o_ref[r0:r0 + RG_CHUNK, :] = (u_ref[r0:r0 + RG_CHUNK, :] * gate).astype(o_ref.dtype)


def _rglru(mixer_in, col0, conv_w, conv_b, w_a, b_a, w_x, b_x, lam):
    wt = RG_TILE
    n_wt = LRU_WIDTH // wt
    c0 = col0 // wt
    nb = wt // LRU_BLOCK_WIDTH
    vec = pl.BlockSpec((1, wt), lambda b, j: (0, j))
    blk = pl.BlockSpec((nb, LRU_BLOCK_WIDTH, LRU_BLOCK_WIDTH), lambda b, j: (j, 0, 0))
    return pl.pallas_call(
        _rglru_kernel,
        name="conv_rglru",
        grid=(BATCH, n_wt),
        in_specs=[
            pl.BlockSpec((SEQ, wt), lambda b, j: (b, c0 + j)),
            pl.BlockSpec((SEQ, wt), lambda b, j: (b, c0 + n_wt + j)),
            pl.BlockSpec((CONV_WIDTH, wt), lambda b, j: (0, j)),
            vec, blk, vec, blk, vec, vec,
        ],
        out_specs=pl.BlockSpec((SEQ, wt), lambda b, j: (b, j)),
        out_shape=jax.ShapeDtypeStruct((TOKENS, LRU_WIDTH), BF16),
        scratch_shapes=[pltpu.VMEM((SEQ + CONV_PAD, wt), F32),
                        pltpu.VMEM((SEQ, wt), F32),
                        pltpu.VMEM((SEQ, wt), F32)],
        compiler_params=_params(("parallel", "parallel")),
    )(mixer_in, mixer_in, conv_w, conv_b, w_a, b_a, w_x, b_x, lam)


def _rope_tables():
    inv_freq = ROPE_THETA ** (-jnp.arange(0, HEAD_DIM, 2, dtype=F32) / HEAD_DIM)
    ang = jnp.arange(SEQ, dtype=F32)[:, None] * inv_freq[None, :]
    cos, sin = jnp.cos(ang), jnp.sin(ang)
    return jnp.concatenate([cos, cos], axis=1), jnp.concatenate([-sin, sin], axis=1)


def _ffn(xn, w_gate, w_up, w_down):
    h, w_down_bf = _mm("ffn_up", [xn], [(w_gate, 0), (w_up, 0)], [(0, 0), (0, 1)], [],
                       _ep_swiglu, [BF16], tm=2048, tn=256, n_cols=D_FF, a_buffers=1,
                       side_cast=w_down, row_parts=2)
    (f,) = _mm("ffn_down", [h], [(w_down_bf, 0)], [(0, 0)], [], _ep_identity, [BF16],
               tm=512, tn=512, n_cols=D_MODEL)
    return f


def kernel(x, ffn1_pre_g, ffn1_w_gate, ffn1_w_up, ffn1_w_down, ffn1_post_g, mix_pre_g, w_in, conv_w, conv_b, rg_w_a, rg_b_a, rg_w_x, rg_b_x, lru_lambda, w_attn_out, w_rec_out, w_o, mix_post_g, ffn2_pre_g, ffn2_w_gate, ffn2_w_up, ffn2_w_down, ffn2_post_g):
    depth = ffn1_pre_g.shape[0]
    cos_t, sin_t = _rope_tables()
    xf = x.reshape(TOKENS, D_MODEL)
    row = lambda p: p.reshape(1, -1)
    c_rec, c_gate = 3 * ATTN_WIDTH, 3 * ATTN_WIDTH + 2 * LRU_WIDTH
    proj = dict(tm=2048, tn=512, a_buffers=1)
    for l in range(depth):
        xn = _prenorm(xf, row(ffn1_pre_g[l]))
        f = _ffn(xn, ffn1_w_gate[l], ffn1_w_up[l], ffn1_w_down[l])
        xf, hn = _residual_norm(xf, f, row(ffn1_post_g[l]), row(mix_pre_g[l]),
                                weight=MACARON_WEIGHT)
        (mixer_in,) = _mm("proj_mixer_in", [hn], [(w_in[l], 0)], [(0, 0)],
                          [('pos', cos_t), ('pos', sin_t)], None, [BF16],
                          n_cols=c_gate, rope_cols=2 * ATTN_WIDTH, **proj)
        (merge_gates,) = _mm("proj_merge_gates", [hn], [(w_in[l], c_gate)], [(0, 0)], [],
                             _ep_sigmoid, [BF16], n_cols=2 * D_MODEL, row_parts=2, **proj)
        attn = _moba_attention(mixer_in)
        y_rec = _rglru(mixer_in, c_rec, conv_w[l], row(conv_b[l]), rg_w_a[l], row(rg_b_a[l]),
                       rg_w_x[l], row(rg_b_x[l]), row(lru_lambda[l]))
        (merged,) = _mm("mixer_out_merge", [attn, y_rec],
                        [(w_attn_out[l], 0), (w_rec_out[l], 0)], [(0, 0), (1, 1)],
                        [('mn', merge_gates, 0), ('mn', merge_gates, D_MODEL)],
                        _ep_merge, [BF16], n_cols=D_MODEL, tm=1024, tn=512)
        (mix,) = _mm("proj_o", [merged], [(w_o[l], 0)], [(0, 0)], [], _ep_identity, [BF16],
                     n_cols=D_MODEL, **proj)
        xf, xn2 = _residual_norm(xf, mix, row(mix_post_g[l]), row(ffn2_pre_g[l]), weight=1.0)
        f = _ffn(xn2, ffn2_w_gate[l], ffn2_w_up[l], ffn2_w_down[l])
        xf = _residual(xf, f, row(ffn2_post_g[l]), weight=MACARON_WEIGHT)
    return xf.reshape(BATCH, SEQ, D_MODEL)
```
